```python
import math
import jax, jax.numpy as jnp
from jax import lax
import numpy as np

D_MODEL = 2048
BATCH = 1
SEQ = 16384
DEPTH = 2

D_FF = 5632
NORM_EPS = 1e-6
NEG = -1e30

A_HEADS = 4
A_DQK = 128
A_DV = 256
A_WIDTH = A_HEADS * A_DV
A_CHUNK = 64

B_HEADS = 8
B_Q_LORA = 384
B_KV_LORA = 256
B_NOPE = 128
B_ROPE = 64
B_DV = 128
B_WIDTH = B_HEADS * B_DV
B_QBLOCK = 128
ROPE_BASE = 10000.0

C_WIDTH = 1024
C_BLOCKS = 8
C_BLOCK_DIM = C_WIDTH // C_BLOCKS
C_CONV = 4
C_POW = 8.0

N_BRANCH = 3
BR_WIDTH = 1024

IN_SPLITS = (A_HEADS * A_DQK, A_HEADS * A_DQK, A_WIDTH, A_HEADS, A_HEADS, A_WIDTH,
             B_Q_LORA, B_KV_LORA, B_ROPE,
             C_WIDTH,
             N_BRANCH * D_MODEL)
N_IN = sum(IN_SPLITS)

kernel_name = "hybrid_mlstm_mla_rglru_gated_macaron"


def rmsnorm(x, g):
    xf = x.astype(jnp.float32)
    y = xf * lax.rsqrt(jnp.mean(xf * xf, axis=-1, keepdims=True) + NORM_EPS)
    return (y * g.astype(jnp.float32)).astype(x.dtype)


def swiglu(x, w_gate, w_up, w_down):
    return (jax.nn.silu(x @ w_gate) * (x @ w_up)) @ w_down


def apply_rope(x, cos, sin):
    xf = x.astype(jnp.float32)
    x1, x2 = jnp.split(xf, 2, axis=-1)
    return jnp.concatenate([x1 * cos - x2 * sin, x1 * sin + x2 * cos], axis=-1).astype(x.dtype)


def _to_chunks(t):
    b, s, h = t.shape[:3]
    t = t.reshape((b, s // A_CHUNK, A_CHUNK, h) + t.shape[3:])
    perm = (1, 0, 3, 2) + tuple(range(4, t.ndim))
    return t.transpose(perm)


def mlstm_chunkwise(q, k, v, i_pre, f_pre):
    b, s, h, _ = q.shape
    qf = q.astype(jnp.float32) * (A_DQK ** -0.5)
    kf = k.astype(jnp.float32)
    vf = v.astype(jnp.float32)
    log_i = i_pre.astype(jnp.float32)
    log_f = jax.nn.log_sigmoid(f_pre.astype(jnp.float32))
    xs = (_to_chunks(qf), _to_chunks(kf), _to_chunks(vf), _to_chunks(log_i), _to_chunks(log_f))
    causal = jnp.tril(jnp.ones((A_CHUNK, A_CHUNK), dtype=bool))

    def step(carry, inp):
        c_prev, n_prev, m_prev = carry
        qc, kc, vc, ic, lfc = inp
        bcum = jnp.cumsum(lfc, axis=-1)
        b_tot = bcum[..., -1]
        d_intra = bcum[..., :, None] - bcum[..., None, :] + ic[..., None, :]
        d_intra = jnp.where(causal, d_intra, NEG)
        d_inter = bcum + m_prev[..., None]
        m_t = jnp.maximum(jnp.max(d_intra, axis=-1), d_inter)
        w_intra = jnp.exp(d_intra - m_t[..., None])
        w_inter = jnp.exp(d_inter - m_t)
        sc = jnp.einsum('bhtd,bhsd->bhts', qc, kc) * w_intra
        num = (jnp.einsum('bhts,bhsv->bhtv', sc, vc)
               + w_inter[..., None] * jnp.einsum('bhtd,bhdv->bhtv', qc, c_prev))
        den = jnp.sum(sc, axis=-1) + w_inter * jnp.einsum('bhtd,bhd->bht', qc, n_prev)
        h_out = num / jnp.maximum(jnp.abs(den), jnp.exp(-m_t))[..., None]
        d_state = b_tot[..., None] - bcum + ic
        m_new = jnp.maximum(b_tot + m_prev, jnp.max(d_state, axis=-1))
        w_state = jnp.exp(d_state - m_new[..., None])
        w_prev = jnp.exp(b_tot + m_prev - m_new)
        c_new = w_prev[..., None, None] * c_prev + jnp.einsum('bhs,bhsd,bhsv->bhdv', w_state, kc, vc)
        n_new = w_prev[..., None] * n_prev + jnp.einsum('bhs,bhsd->bhd', w_state, kc)
        return (c_new, n_new, m_new), h_out

    init = (jnp.zeros((b, h, A_DQK, A_DV), jnp.float32),
            jnp.zeros((b, h, A_DQK), jnp.float32),
            jnp.full((b, h), NEG, jnp.float32))
    _, hs = lax.scan(step, init, xs)
    return hs.transpose(1, 0, 3, 2, 4).reshape(b, s, h, A_DV)


def mla_attention(q, k, v):
    b, s, h, dk = q.shape
    nb = s // B_QBLOCK
    scale = (B_NOPE + B_ROPE) ** -0.5
    qb = q.reshape(b, nb, B_QBLOCK, h, dk).transpose(1, 0, 2, 3, 4)
    kpos = jnp.arange(s)

    def block(args):
        qblk, start = args
        sc = jnp.einsum('bqhd,bkhd->bhqk', qblk, k).astype(jnp.float32) * scale
        qpos = start + jnp.arange(B_QBLOCK)
        sc = jnp.where(kpos[None, :] <= qpos[:, None], sc, NEG)
        p = jax.nn.softmax(sc, axis=-1)
        return jnp.einsum('bhqk,bkhd->bqhd', p.astype(v.dtype), v)

    out = lax.map(block, (qb, jnp.arange(nb) * B_QBLOCK))
    return out.transpose(1, 0, 2, 3, 4).reshape(b, s, h * B_DV)


def causal_depthwise_conv(x, w, bias):
    y = lax.conv_general_dilated(x, w[:, None, :].astype(x.dtype), window_strides=(1,),
                                 padding=[(C_CONV - 1, 0)],
                                 dimension_numbers=('NWC', 'WIO', 'NWC'),
                                 feature_group_count=x.shape[-1])
    return y + bias


def rg_lru(x, w_a, b_a, w_x, b_x, lam):
    b, s, c = x.shape
    xb = x.reshape(b, s, C_BLOCKS, C_BLOCK_DIM)
    r = jax.nn.sigmoid(jnp.einsum('bsnd,nde->bsne', xb, w_a).reshape(b, s, c) + b_a).astype(jnp.float32)
    gi = jax.nn.sigmoid(jnp.einsum('bsnd,nde->bsne', xb, w_x).reshape(b, s, c) + b_x).astype(jnp.float32)
    log_a = -C_POW * r * jax.nn.softplus(-lam.astype(jnp.float32))
    a = jnp.exp(log_a)
    u = jnp.sqrt(-jnp.expm1(2.0 * log_a)) * (gi * x.astype(jnp.float32))

    def combine(left, right):
        a1, b1 = left
        a2, b2 = right
        return a1 * a2, a2 * b1 + b2

    _, h = lax.associative_scan(combine, (a, u), axis=1)
    return h.astype(x.dtype)


def hybrid_layer(x, cos, sin, ffn1_norm, ffn1_w_gate, ffn1_w_up, ffn1_w_down, mix_norm, w_in,
                 mlstm_gate_bias, mlstm_out_norm, mla_q_norm, mla_w_uq, mla_kv_norm, mla_w_ukv,
                 lru_conv_w, lru_conv_b, lru_w_a, lru_b_a, lru_w_x, lru_b_x, lru_lambda,
                 w_branch, w_out, ffn2_norm, ffn2_w_gate, ffn2_w_up, ffn2_w_down):
    b, s, _ = x.shape
    x = x + 0.5 * swiglu(rmsnorm(x, ffn1_norm), ffn1_w_gate, ffn1_w_up, ffn1_w_down)

    u = rmsnorm(x, mix_norm)
    proj = u @ w_in
    offsets = np.cumsum(IN_SPLITS)[:-1].tolist()
    (a_q, a_k, a_v, a_i, a_f, a_o, b_cq, b_ckv, b_kr, c_x, gates) = jnp.split(proj, offsets, axis=-1)

    h_a = mlstm_chunkwise(a_q.reshape(b, s, A_HEADS, A_DQK), a_k.reshape(b, s, A_HEADS, A_DQK),
                          a_v.reshape(b, s, A_HEADS, A_DV),
                          a_i + mlstm_gate_bias[:A_HEADS], a_f + mlstm_gate_bias[A_HEADS:])
    h_a = rmsnorm(h_a, mlstm_out_norm.reshape(A_HEADS, A_DV)).reshape(b, s, A_WIDTH)
    y_a = (jax.nn.sigmoid(a_o.astype(jnp.float32)) * h_a).astype(x.dtype)

    q = (rmsnorm(b_cq, mla_q_norm) @ mla_w_uq).reshape(b, s, B_HEADS, B_NOPE + B_ROPE)
    q_nope, q_rope = jnp.split(q, [B_NOPE], axis=-1)
    q_rope = apply_rope(q_rope, cos[None, :, None, :], sin[None, :, None, :])
    kv = (rmsnorm(b_ckv, mla_kv_norm) @ mla_w_ukv).reshape(b, s, B_HEADS, B_NOPE + B_DV)
    k_nope, v_b = jnp.split(kv, [B_NOPE], axis=-1)
    k_rope = apply_rope(b_kr, cos[None], sin[None])
    k_b = jnp.concatenate([k_nope, jnp.broadcast_to(k_rope[:, :, None, :], (b, s, B_HEADS, B_ROPE))], axis=-1)
    q_b = jnp.concatenate([q_nope, q_rope], axis=-1)
    y_b = mla_attention(q_b, k_b, v_b)

    xc = causal_depthwise_conv(c_x, lru_conv_w, lru_conv_b)
    y_c = rg_lru(xc, lru_w_a, lru_b_a, lru_w_x, lru_b_x, lru_lambda)

    branches = jnp.stack([y_a, y_b, y_c], axis=2)
    proj_b = jnp.einsum('bsjc,jcd->bsjd', branches, w_branch)
    g = jax.nn.sigmoid(gates.reshape(b, s, N_BRANCH, D_MODEL))
    z = jnp.sum(g * proj_b, axis=2)
    x = x + z @ w_out

    x = x + 0.5 * swiglu(rmsnorm(x, ffn2_norm), ffn2_w_gate, ffn2_w_up, ffn2_w_down)
    return x


def setup_inputs(seed: int = 0) -> dict:
    key = jax.random.key(seed)
    ks = jax.random.split(key, 32)
    f32 = jnp.float32
    L = DEPTH

    def nrm(k, shape, fan_in):
        return jax.random.normal(k, shape, f32) * (fan_in ** -0.5)

    def gain(k, shape):
        return 1.0 + 0.01 * jax.random.normal(k, shape, f32)

    f_bias = jnp.linspace(3.0, 6.0, A_HEADS, dtype=f32)[None, :] + 0.01 * jax.random.normal(ks[7], (L, A_HEADS), f32)
    i_bias = 0.1 * jax.random.normal(ks[8], (L, A_HEADS), f32)
    a8 = jax.random.uniform(ks[20], (L, C_WIDTH), f32, 0.9, 0.999)
    a_base = a8 ** (1.0 / C_POW)
    lam = jnp.log(a_base) - jnp.log1p(-a_base)
    return {
        "x": jax.random.normal(ks[0], (BATCH, SEQ, D_MODEL), f32),
        "ffn1_norm": gain(ks[1], (L, D_MODEL)),
        "ffn1_w_gate": nrm(ks[2], (L, D_MODEL, D_FF), D_MODEL),
        "ffn1_w_up": nrm(ks[3], (L, D_MODEL, D_FF), D_MODEL),
        "ffn1_w_down": nrm(ks[4], (L, D_FF, D_MODEL), D_FF),
        "mix_norm": gain(ks[5], (L, D_MODEL)),
        "w_in": nrm(ks[6], (L, D_MODEL, N_IN), D_MODEL),
        "mlstm_gate_bias": jnp.concatenate([i_bias, f_bias], axis=-1),
        "mlstm_out_norm": gain(ks[9], (L, A_WIDTH)),
        "mla_q_norm": gain(ks[10], (L, B_Q_LORA)),
        "mla_w_uq": nrm(ks[11], (L, B_Q_LORA, B_HEADS * (B_NOPE + B_ROPE)), B_Q_LORA),
        "mla_kv_norm": gain(ks[12], (L, B_KV_LORA)),
        "mla_w_ukv": nrm(ks[13], (L, B_KV_LORA, B_HEADS * (B_NOPE + B_DV)), B_KV_LORA),
        "lru_conv_w": nrm(ks[14], (L, C_CONV, C_WIDTH), C_CONV),
        "lru_conv_b": 0.01 * jax.random.normal(ks[15], (L, C_WIDTH), f32),
        "lru_w_a": nrm(ks[16], (L, C_BLOCKS, C_BLOCK_DIM, C_BLOCK_DIM), C_BLOCK_DIM),
        "lru_b_a": 0.01 * jax.random.normal(ks[17], (L, C_WIDTH), f32),
        "lru_w_x": nrm(ks[18], (L, C_BLOCKS, C_BLOCK_DIM, C_BLOCK_DIM), C_BLOCK_DIM),
        "lru_b_x": 0.01 * jax.random.normal(ks[19], (L, C_WIDTH), f32),
        "lru_lambda": lam,
        "w_branch": nrm(ks[21], (L, N_BRANCH, BR_WIDTH, D_MODEL), BR_WIDTH),
        "w_out": nrm(ks[22], (L, D_MODEL, D_MODEL), D_MODEL),
        "ffn2_norm": gain(ks[23], (L, D_MODEL)),
        "ffn2_w_gate": nrm(ks[24], (L, D_MODEL, D_FF), D_MODEL),
        "ffn2_w_up": nrm(ks[25], (L, D_MODEL, D_FF), D_MODEL),
        "ffn2_w_down": nrm(ks[26], (L, D_FF, D_MODEL), D_FF),
        "final_norm": gain(ks[27], (D_MODEL,)),
    }


def reference(x, ffn1_norm, ffn1_w_gate, ffn1_w_up, ffn1_w_down, mix_norm, w_in,
              mlstm_gate_bias, mlstm_out_norm, mla_q_norm, mla_w_uq, mla_kv_norm, mla_w_ukv,
              lru_conv_w, lru_conv_b, lru_w_a, lru_b_a, lru_w_x, lru_b_x, lru_lambda,
              w_branch, w_out, ffn2_norm, ffn2_w_gate, ffn2_w_up, ffn2_w_down, final_norm):
    s = x.shape[1]
    pos = jnp.arange(s, dtype=jnp.float32)
    inv_freq = jnp.power(ROPE_BASE, -jnp.arange(0, B_ROPE, 2, dtype=jnp.float32) / B_ROPE)
    ang = pos[:, None] * inv_freq[None, :]
    cos, sin = jnp.cos(ang), jnp.sin(ang)
    for l in range(DEPTH):
        x = hybrid_layer(x, cos, sin, ffn1_norm[l], ffn1_w_gate[l], ffn1_w_up[l], ffn1_w_down[l],
                         mix_norm[l], w_in[l], mlstm_gate_bias[l], mlstm_out_norm[l],
                         mla_q_norm[l], mla_w_uq[l], mla_kv_norm[l], mla_w_ukv[l],
                         lru_conv_w[l], lru_conv_b[l], lru_w_a[l], lru_b_a[l], lru_w_x[l], lru_b_x[l],
                         lru_lambda[l], w_branch[l], w_out[l],
                         ffn2_norm[l], ffn2_w_gate[l], ffn2_w_up[l], ffn2_w_down[l])
    return rmsnorm(x, final_norm)
```

```python
import functools

import jax
import jax.numpy as jnp
from jax import lax
from jax.experimental import pallas as pl
from jax.experimental.pallas import tpu as pltpu

F32 = jnp.float32
BF16 = jnp.bfloat16

D_MODEL = 2048
D_FF = 5632
NORM_EPS = 1e-6
NEG = -1e30

A_HEADS = 4
A_DQK = 128
A_DV = 256
A_WIDTH = A_HEADS * A_DV

B_HEADS = 8
B_Q_LORA = 384
B_KV_LORA = 256
B_NOPE = 128
B_ROPE = 64
B_DV = 128
B_DK_PAD = 256
ROPE_BASE = 10000.0

C_WIDTH = 1024
C_BLOCKS = 8
C_BLOCK_DIM = C_WIDTH // C_BLOCKS
C_CONV = 4
C_POW = 8.0

N_BRANCH = 3

_OFF_Q = 0
_OFF_K = 512
_OFF_V = 1024
_OFF_I = 2048
_OFF_O = 2056
_OFF_CQ = 3080
_OFF_CKV = 3464
_OFF_KR = 3720
_OFF_CX = 3784
_OFF_G = 4808

VMEM_LIMIT_BYTES = 56 * 1024 * 1024


def _cparams(*sem):
    return pltpu.CompilerParams(dimension_semantics=sem, vmem_limit_bytes=VMEM_LIMIT_BYTES)


def _sigmoid(x):
    return 1.0 / (1.0 + jnp.exp(-x))


def _softplus(x):
    return jnp.maximum(x, 0.0) + jnp.log1p(jnp.exp(-jnp.abs(x)))


def _rms(x, g):
    return x * lax.rsqrt(jnp.mean(x * x, axis=-1, keepdims=True) + NORM_EPS) * g


def _dot(a, b, **kw):
    return jnp.dot(a, b, preferred_element_type=F32, **kw)


def _dot_nt(a, b):
    return lax.dot_general(a, b, (((1,), (1,)), ((), ())), preferred_element_type=F32)


def _dot_tn(a, b):
    return lax.dot_general(a, b, (((0,), (0,)), ((), ())), preferred_element_type=F32)


def _ffn_body(x_ref, g_ref, wg_ref, wu_ref, wd_ref, g2_ref, *rest, n_ff, mode):
    if mode == "mix":
        o_ref, u_ref, xn_ref = rest
    else:
        o_ref, xn_ref = rest
    j = pl.program_id(1)

    @pl.when(j == 0)
    def _():
        xn_ref[...] = _rms(x_ref[...], g_ref[...]).astype(BF16)
        o_ref[...] = jnp.zeros_like(o_ref)

    xn = xn_ref[...]
    h1 = _dot(xn, wg_ref[...])
    h2 = _dot(xn, wu_ref[...])
    h = (h1 * _sigmoid(h1) * h2).astype(BF16)
    o_ref[...] += _dot(h, wd_ref[...])

    @pl.when(j == n_ff - 1)
    def _():
        y = x_ref[...] + 0.5 * o_ref[...]
        if mode == "final":
            o_ref[...] = _rms(y, g2_ref[...])
        else:
            o_ref[...] = y
        if mode == "mix":
            u_ref[...] = _rms(y, g2_ref[...]).astype(BF16)


def _ffn(x, g, wg, wu, wd, g2, *, mode, tm=512, tf=512):
    s, d = x.shape
    n_ff = wg.shape[1] // tf
    out_shape = [jax.ShapeDtypeStruct((s, d), F32)]
    out_specs = [pl.BlockSpec((tm, d), lambda i, j: (i, 0))]
    if mode == "mix":
        out_shape.append(jax.ShapeDtypeStruct((s, d), BF16))
        out_specs.append(pl.BlockSpec((tm, d), lambda i, j: (i, 0)))
    res = pl.pallas_call(
        functools.partial(_ffn_body, n_ff=n_ff, mode=mode),
        grid=(s // tm, n_ff),
        in_specs=[
            pl.BlockSpec((tm, d), lambda i, j: (i, 0)),
            pl.BlockSpec((1, d), lambda i, j: (0, 0)),
            pl.BlockSpec((d, tf), lambda i, j: (0, j)),
            pl.BlockSpec((d, tf), lambda i, j: (0, j)),
            pl.BlockSpec((tf, d), lambda i, j: (j, 0)),
            pl.BlockSpec((1, d), lambda i, j: (0, 0)),
        ],
        out_specs=out_specs,
        out_shape=out_shape,
        scratch_shapes=[pltpu.VMEM((tm, d), BF16)],
        compiler_params=_cparams("parallel", "arbitrary"),
        name="ffn_" + mode,
    )(x, g, wg, wu, wd, g2)
    return res if mode == "mix" else res[0]


def _mm_body(a_ref, w_ref, o_ref):
    o_ref[...] = _dot(a_ref[...], w_ref[...]).astype(o_ref.dtype)


def _matmul(a, w, out_dtype, *, tm=1024, tn=512, name="proj"):
    s, k = a.shape
    n = w.shape[1]
    tn = min(tn, n)
    return pl.pallas_call(
        _mm_body,
        grid=(s // tm, n // tn),
        in_specs=[pl.BlockSpec((tm, k), lambda i, j: (i, 0)),
                  pl.BlockSpec((k, tn), lambda i, j: (0, j))],
        out_specs=pl.BlockSpec((tm, tn), lambda i, j: (i, j)),
        out_shape=jax.ShapeDtypeStruct((s, n), out_dtype),
        compiler_params=_cparams("parallel", "arbitrary"),
        name=name,
    )(a, w)


def _log_sigmoid(x):
    return jnp.minimum(x, 0.0) - jnp.log1p(jnp.exp(-jnp.abs(x)))


def _gates_body(u_ref, wc_ref, wr_ref, bc_ref, br_ref, gc_ref, gr_ref):
    u = u_ref[...]
    gc = _dot(u, wc_ref[...]) + bc_ref[...]
    gr = _dot_nt(wr_ref[...], u) + br_ref[...]
    lane = lax.broadcasted_iota(jnp.int32, gc.shape, 1)
    row = lax.broadcasted_iota(jnp.int32, gr.shape, 0)
    gc_ref[...] = jnp.where(lane >= A_HEADS, _log_sigmoid(gc), gc)
    gr_ref[...] = jnp.where(row >= A_HEADS, _log_sigmoid(gr), gr)


def _gates(u, wc, wr, bc, br, *, tm=1024):
    s, d = u.shape
    return pl.pallas_call(
        _gates_body,
        grid=(s // tm,),
        in_specs=[pl.BlockSpec((tm, d), lambda i: (i, 0)),
                  pl.BlockSpec((d, 128), lambda i: (0, 0)),
                  pl.BlockSpec((16, d), lambda i: (0, 0)),
                  pl.BlockSpec((1, 128), lambda i: (0, 0)),
                  pl.BlockSpec((16, 1), lambda i: (0, 0))],
        out_specs=[pl.BlockSpec((tm, 128), lambda i: (i, 0)),
                   pl.BlockSpec((16, tm), lambda i: (0, i))],
        out_shape=[jax.ShapeDtypeStruct((s, 128), F32),
                   jax.ShapeDtypeStruct((16, s), F32)],
        compiler_params=_cparams("parallel"),
        name="mlstm_gates",
    )(u, wc, wr, bc, br)


def _mlstm_body(q_ref, k_ref, v_ref, gc_ref, gr_ref, o_ref, nw_ref, y_ref,
                c_ref, n_ref, m_ref, *, chunk):
    L = chunk

    @pl.when(pl.program_id(0) == 0)
    def _():
        c_ref[...] = jnp.zeros_like(c_ref)
        n_ref[...] = jnp.zeros_like(n_ref)
        m_ref[...] = jnp.full_like(m_ref, NEG)

    row = lax.broadcasted_iota(jnp.int32, (L, L), 0)
    col = lax.broadcasted_iota(jnp.int32, (L, L), 1)
    causal = col <= row
    tril = causal.astype(F32)
    triu = (row <= col).astype(F32)
    gc = gc_ref[...]
    gr = gr_ref[...]
    cum_c = _dot(tril, gc, precision=lax.Precision.HIGHEST)
    cum_r = _dot(gr, triu, precision=lax.Precision.HIGHEST)

    for h in range(A_HEADS):
        bcol = cum_c[:, A_HEADS + h:A_HEADS + h + 1]
        brow = cum_r[A_HEADS + h:A_HEADS + h + 1, :]
        icol = gc[:, h:h + 1]
        irow = gr[h:h + 1, :]
        btot = brow[:, L - 1:L]
        m_prev = m_ref[h][0:1, 0:1]

        d_intra = jnp.where(causal, bcol - brow + irow, NEG)
        d_inter = bcol + m_prev
        m_t = jnp.maximum(jnp.max(d_intra, axis=1, keepdims=True), d_inter)
        w_intra = jnp.exp(d_intra - m_t)
        w_inter = jnp.exp(d_inter - m_t)

        qh = q_ref[:, h * A_DQK:(h + 1) * A_DQK]
        kh = k_ref[:, h * A_DQK:(h + 1) * A_DQK]
        vh = v_ref[:, h * A_DV:(h + 1) * A_DV]
        c_prev = c_ref[h]
        n_prev = n_ref[h][0:1, :]

        sc = _dot_nt(qh, kh) * w_intra
        num = _dot(sc.astype(BF16), vh) + w_inter * _dot(qh, c_prev.astype(BF16))
        qn = jnp.sum(qh.astype(F32) * n_prev, axis=1, keepdims=True)
        den = jnp.sum(sc, axis=1, keepdims=True) + w_inter * qn
        h_out = num / jnp.maximum(jnp.abs(den), jnp.exp(-m_t))

        d_state_r = btot - brow + irow
        m_new = jnp.maximum(btot + m_prev, jnp.max(d_state_r, axis=1, keepdims=True))
        w_state = jnp.exp(btot - bcol + icol - m_new)
        w_prev = jnp.exp(btot + m_prev - m_new)
        kw = kh.astype(F32) * w_state
        c_ref[h] = w_prev * c_prev + _dot_tn(kw.astype(BF16), vh)
        n_new = w_prev * n_prev + jnp.sum(kw, axis=0, keepdims=True)
        n_ref[h] = jnp.broadcast_to(n_new, n_ref.shape[1:])
        m_ref[h] = jnp.broadcast_to(m_new, m_ref.shape[1:])

        sl = slice(h * A_DV, (h + 1) * A_DV)
        hn = _rms(h_out, nw_ref[:, sl])
        y_ref[:, sl] = (_sigmoid(o_ref[:, sl]) * hn).astype(BF16)


def _mlstm(p1, gc, gr, p2, nw, *, chunk=256):
    s = p1.shape[0]
    L = chunk
    return pl.pallas_call(
        functools.partial(_mlstm_body, chunk=L),
        grid=(s // L,),
        in_specs=[pl.BlockSpec((L, 512), lambda c: (c, 0)),
                  pl.BlockSpec((L, 512), lambda c: (c, 1)),
                  pl.BlockSpec((L, 1024), lambda c: (c, 1)),
                  pl.BlockSpec((L, 128), lambda c: (c, 0)),
                  pl.BlockSpec((16, L), lambda c: (0, c)),
                  pl.BlockSpec((L, 1024), lambda c: (c, 0)),
                  pl.BlockSpec((1, A_WIDTH), lambda c: (0, 0))],
        out_specs=pl.BlockSpec((L, A_WIDTH), lambda c: (c, 0)),
        out_shape=jax.ShapeDtypeStruct((s, A_WIDTH), BF16),
        scratch_shapes=[pltpu.VMEM((A_HEADS, A_DQK, A_DV), F32),
                        pltpu.VMEM((A_HEADS, 8, A_DQK), F32),
                        pltpu.VMEM((A_HEADS, 8, 128), F32)],
        compiler_params=_cparams("arbitrary"),
        name="mlstm",
    )(p1, p1, p1, gc, gr, p2, nw)


def _mla_prep_body(p3_ref, qn_ref, kvn_ref, wq_ref, wkv_ref, cs_ref, q_out, k_out, v_out):
    p3 = p3_ref[...]
    cq = _rms(p3[:, 0:B_Q_LORA], qn_ref[...]).astype(BF16)
    ckv = _rms(p3[:, 512:768], kvn_ref[...]).astype(BF16)
    cs = cs_ref[...]
    low = lax.broadcasted_iota(jnp.int32, cs.shape, 1) < B_ROPE

    def rope(t):
        tt = t * cs
        return jnp.where(low, tt + pltpu.roll(tt, B_ROPE, axis=1), 0.0)

    qall = _dot(cq, wq_ref[...])
    kv = _dot(ckv, wkv_ref[...])
    kr = rope(p3[:, 384:512]).astype(BF16)
    for h in range(B_HEADS):
        q_out[h, :, 0:128] = qall[:, h * 128:(h + 1) * 128].astype(BF16)
        q_out[h, :, 128:256] = rope(qall[:, 1024 + h * 128:1024 + (h + 1) * 128]).astype(BF16)
        k_out[h, :, 0:128] = kv[:, h * 128:(h + 1) * 128].astype(BF16)
        k_out[h, :, 128:256] = kr
        v_out[h] = kv[:, 1024 + h * 128:1024 + (h + 1) * 128].astype(BF16)


def _mla_prep(p3, qn, kvn, wq, wkv, cs, *, tm=512):
    s = p3.shape[0]
    return pl.pallas_call(
        _mla_prep_body,
        grid=(s // tm,),
        in_specs=[pl.BlockSpec((tm, 768), lambda i: (i, 0)),
                  pl.BlockSpec((1, B_Q_LORA), lambda i: (0, 0)),
                  pl.BlockSpec((1, B_KV_LORA), lambda i: (0, 0)),
                  pl.BlockSpec((B_Q_LORA, 2048), lambda i: (0, 0)),
                  pl.BlockSpec((B_KV_LORA, 2048), lambda i: (0, 0)),
                  pl.BlockSpec((tm, 128), lambda i: (i, 0))],
        out_specs=[pl.BlockSpec((B_HEADS, tm, B_DK_PAD), lambda i: (0, i, 0)),
                   pl.BlockSpec((B_HEADS, tm, B_DK_PAD), lambda i: (0, i, 0)),
                   pl.BlockSpec((B_HEADS, tm, B_DV), lambda i: (0, i, 0))],
        out_shape=[jax.ShapeDtypeStruct((B_HEADS, s, B_DK_PAD), BF16),
                   jax.ShapeDtypeStruct((B_HEADS, s, B_DK_PAD), BF16),
                   jax.ShapeDtypeStruct((B_HEADS, s, B_DV), BF16)],
        compiler_params=_cparams("parallel"),
        name="mla_prep",
    )(p3, qn, kvn, wq, wkv, cs)


def _flash_body(q_ref, k_ref, v_ref, o_ref, *, tq, tk):
    qi = pl.program_id(1)
    q = q_ref[...]

    def step(j, carry, masked):
        m, l, acc = carry
        off = pl.multiple_of(j * tk, tk)
        k = k_ref[pl.ds(off, tk), :]
        v = v_ref[pl.ds(off, tk), :]
        s = _dot_nt(q, k)
        if masked:
            rows = lax.broadcasted_iota(jnp.int32, (tq, tk), 0) + qi * tq
            cols = lax.broadcasted_iota(jnp.int32, (tq, tk), 1) + j * tk
            s = jnp.where(cols <= rows, s, NEG)
        m_new = jnp.maximum(m, jnp.max(s, axis=1, keepdims=True))
        p = jnp.exp(s - m_new)
        a = jnp.exp(m - m_new)
        l = a * l + jnp.sum(p, axis=1, keepdims=True)
        acc = a * acc + _dot(p.astype(BF16), v)
        return m_new, l, acc

    init = (jnp.full((tq, 1), NEG, F32), jnp.zeros((tq, 1), F32), jnp.zeros((tq, B_DV), F32))
    n_diag = tq // tk
    n_full = qi * n_diag
    carry = lax.fori_loop(0, n_full, lambda j, c: step(j, c, False), init)
    for d in range(n_diag):
        carry = step(n_full + d, carry, True)
    m, l, acc = carry
    o_ref[...] = (acc / l).astype(BF16)


def _flash(q, k, v, *, tq=512, tk=512):
    nh, s, dk = q.shape
    return pl.pallas_call(
        functools.partial(_flash_body, tq=tq, tk=tk),
        grid=(nh, s // tq),
        in_specs=[pl.BlockSpec((None, tq, dk), lambda h, i: (h, i, 0)),
                  pl.BlockSpec((None, s, dk), lambda h, i: (h, 0, 0)),
                  pl.BlockSpec((None, s, B_DV), lambda h, i: (h, 0, 0))],
        out_specs=pl.BlockSpec((tq, B_DV), lambda h, i: (i, h)),
        out_shape=jax.ShapeDtypeStruct((s, nh * B_DV), BF16),
        compiler_params=_cparams("parallel", "arbitrary"),
        name="mla_flash",
    )(q, k, v)


def _lru_body(cx_ref, cw_ref, cb_ref, wax_ref, ba_ref, bx_ref, lam_ref, y_ref,
              xs_ref, h_ref, *, tb):
    T = tb

    @pl.when(pl.program_id(0) == 0)
    def _():
        xs_ref[0:8, :] = jnp.zeros((8, C_WIDTH), F32)
        h_ref[...] = jnp.zeros_like(h_ref)

    x = cx_ref[...]
    xs_ref[8:8 + T, :] = x
    w = cw_ref[...]
    xc = (w[3:4, :] * x + w[2:3, :] * xs_ref[7:7 + T, :] + w[1:2, :] * xs_ref[6:6 + T, :]
          + w[0:1, :] * xs_ref[5:5 + T, :] + cb_ref[...])
    xs_ref[0:8, :] = x[T - 8:T, :]

    ra, rx = [], []
    for b in range(C_BLOCKS):
        xb = xc[:, b * C_BLOCK_DIM:(b + 1) * C_BLOCK_DIM].astype(BF16)
        g = _dot(xb, wax_ref[b])
        ra.append(g[:, :C_BLOCK_DIM])
        rx.append(g[:, C_BLOCK_DIM:])
    r = _sigmoid(jnp.concatenate(ra, axis=1) + ba_ref[...])
    gi = _sigmoid(jnp.concatenate(rx, axis=1) + bx_ref[...])
    log_a = -C_POW * r * _softplus(-lam_ref[...])
    a = jnp.exp(log_a)
    u = jnp.sqrt(-jnp.tanh(log_a) * (a * a + 1.0)) * (gi * xc)

    rowi = lax.broadcasted_iota(jnp.int32, (T, C_WIDTH), 0)
    d = 1
    while d < T:
        keep = rowi >= d
        a_s = jnp.where(keep, pltpu.roll(a, d, axis=0), 1.0)
        u_s = jnp.where(keep, pltpu.roll(u, d, axis=0), 0.0)
        u = a * u_s + u
        a = a * a_s
        d *= 2
    hh = a * h_ref[0:1, :] + u
    h_ref[...] = jnp.broadcast_to(hh[T - 1:T, :], h_ref.shape)
    y_ref[...] = hh.astype(BF16)


def _lru(p2, cw, cb, wax, ba, bx, lam, *, tb=256):
    s = p2.shape[0]
    vec = pl.BlockSpec((1, C_WIDTH), lambda c: (0, 0))
    return pl.pallas_call(
        functools.partial(_lru_body, tb=tb),
        grid=(s // tb,),
        in_specs=[pl.BlockSpec((tb, C_WIDTH), lambda c: (c, 1)),
                  pl.BlockSpec((C_CONV, C_WIDTH), lambda c: (0, 0)),
                  vec,
                  pl.BlockSpec((C_BLOCKS, C_BLOCK_DIM, 2 * C_BLOCK_DIM), lambda c: (0, 0, 0)),
                  vec, vec, vec],
        out_specs=pl.BlockSpec((tb, C_WIDTH), lambda c: (c, 0)),
        out_shape=jax.ShapeDtypeStruct((s, C_WIDTH), BF16),
        scratch_shapes=[pltpu.VMEM((tb + 8, C_WIDTH), F32),
                        pltpu.VMEM((8, C_WIDTH), F32)],
        compiler_params=_cparams("arbitrary"),
        name="rglru",
    )(p2, cw, cb, wax, ba, bx, lam)


def _merge_body(x_ref, u_ref, ya_ref, yb_ref, yc_ref, wg_ref, wb_ref, wo_ref, o_ref, *, n_tiles):
    n = pl.program_id(1)

    @pl.when(n == 0)
    def _():
        o_ref[...] = jnp.zeros_like(o_ref)

    u = u_ref[...]
    z = None
    for j, y_ref in enumerate((ya_ref, yb_ref, yc_ref)):
        t = _sigmoid(_dot(u, wg_ref[j])) * _dot(y_ref[...], wb_ref[j])
        z = t if z is None else z + t
    o_ref[...] += _dot(z.astype(BF16), wo_ref[...])

    @pl.when(n == n_tiles - 1)
    def _():
        o_ref[...] = x_ref[...] + o_ref[...]


def _merge(x, u, ya, yb, yc, wg, wb, wo, *, tm=512, tn=512):
    s, d = x.shape
    n_tiles = d // tn
    br = pl.BlockSpec((tm, 1024), lambda i, n: (i, 0))
    return pl.pallas_call(
        functools.partial(_merge_body, n_tiles=n_tiles),
        grid=(s // tm, n_tiles),
        in_specs=[pl.BlockSpec((tm, d), lambda i, n: (i, 0)),
                  pl.BlockSpec((tm, d), lambda i, n: (i, 0)),
                  br, br, br,
                  pl.BlockSpec((N_BRANCH, d, tn), lambda i, n: (0, 0, n)),
                  pl.BlockSpec((N_BRANCH, 1024, tn), lambda i, n: (0, 0, n)),
                  pl.BlockSpec((tn, d), lambda i, n: (n, 0))],
        out_specs=pl.BlockSpec((tm, d), lambda i, n: (i, 0)),
        out_shape=jax.ShapeDtypeStruct((s, d), F32),
        compiler_params=_cparams("parallel", "arbitrary"),
        name="merge",
    )(x, u, ya, yb, yc, wg, wb, wo)


def _rot_cols(w):
    half = B_ROPE // 2
    return jnp.concatenate([-w[..., half:], w[..., :half]], axis=-1)


def _prep_layer(w_in, gate_bias, w_uq, w_ukv, lru_w_a, lru_w_x, w_branch, w_out):
    d = w_in.shape[0]
    w1 = jnp.concatenate([w_in[:, _OFF_Q:_OFF_K] * (A_DQK ** -0.5), w_in[:, _OFF_K:_OFF_I]], axis=1).astype(BF16)
    w2 = jnp.concatenate([w_in[:, _OFF_O:_OFF_CQ], w_in[:, _OFF_CX:_OFF_G]], axis=1).astype(BF16)
    w_kr = w_in[:, _OFF_KR:_OFF_CX]
    w3 = jnp.concatenate([w_in[:, _OFF_CQ:_OFF_CKV], w_kr, _rot_cols(w_kr),
                          w_in[:, _OFF_CKV:_OFF_KR]], axis=1).astype(BF16)
    w_if = w_in[:, _OFF_I:_OFF_O]
    wc = jnp.concatenate([w_if, jnp.zeros((d, 120), F32)], axis=1).astype(BF16)
    wr = jnp.concatenate([w_if.T, jnp.zeros((8, d), F32)], axis=0).astype(BF16)
    bc = jnp.concatenate([gate_bias, jnp.zeros((120,), F32)])[None, :]
    br = jnp.concatenate([gate_bias, jnp.zeros((8,), F32)])[:, None]

    scale = (B_NOPE + B_ROPE) ** -0.5
    wq3 = w_uq.reshape(B_Q_LORA, B_HEADS, B_NOPE + B_ROPE)
    q_nope = wq3[:, :, :B_NOPE].reshape(B_Q_LORA, B_HEADS * B_NOPE)
    q_rope = wq3[:, :, B_NOPE:]
    q_rr = jnp.concatenate([q_rope, _rot_cols(q_rope)], axis=-1).reshape(B_Q_LORA, B_HEADS * 128)
    wq = (jnp.concatenate([q_nope, q_rr], axis=1) * scale).astype(BF16)
    wkv3 = w_ukv.reshape(B_KV_LORA, B_HEADS, B_NOPE + B_DV)
    wkv = jnp.concatenate([wkv3[:, :, :B_NOPE].reshape(B_KV_LORA, -1),
                           wkv3[:, :, B_NOPE:].reshape(B_KV_LORA, -1)], axis=1).astype(BF16)
    wax = jnp.concatenate([lru_w_a, lru_w_x], axis=-1).astype(BF16)
    wg = w_in[:, _OFF_G:].reshape(d, N_BRANCH, D_MODEL).transpose(1, 0, 2).astype(BF16)
    return dict(w1=w1, w2=w2, w3=w3, wc=wc, wr=wr, bc=bc, br=br, wq=wq, wkv=wkv, wax=wax,
                wg=wg, wb=w_branch.astype(BF16), wo=w_out.astype(BF16))


def kernel(x, ffn1_norm, ffn1_w_gate, ffn1_w_up, ffn1_w_down, mix_norm, w_in, mlstm_gate_bias, mlstm_out_norm, mla_q_norm, mla_w_uq, mla_kv_norm, mla_w_ukv, lru_conv_w, lru_conv_b, lru_w_a, lru_b_a, lru_w_x, lru_b_x, lru_lambda, w_branch, w_out, ffn2_norm, ffn2_w_gate, ffn2_w_up, ffn2_w_down, final_norm):
    b, s, d = x.shape
    depth = w_in.shape[0]
    pos = jnp.arange(s, dtype=F32)
    inv_freq = jnp.power(ROPE_BASE, -jnp.arange(0, B_ROPE, 2, dtype=F32) / B_ROPE)
    ang = pos[:, None] * inv_freq[None, :]
    cos, sin = jnp.cos(ang), jnp.sin(ang)
    cs = jnp.concatenate([cos, cos, sin, sin], axis=1)

    outs = []
    for bi in range(b):
        xb = x[bi]
        for l in range(depth):
            w = _prep_layer(w_in[l], mlstm_gate_bias[l], mla_w_uq[l], mla_w_ukv[l],
                            lru_w_a[l], lru_w_x[l], w_branch[l], w_out[l])
            xb, u = _ffn(xb, ffn1_norm[l][None], ffn1_w_gate[l].astype(BF16), ffn1_w_up[l].astype(BF16),
                         ffn1_w_down[l].astype(BF16), mix_norm[l][None], mode="mix")
            p1 = _matmul(u, w["w1"], BF16, name="proj_qkv")
            p2 = _matmul(u, w["w2"], F32, name="proj_o_cx")
            p3 = _matmul(u, w["w3"], F32, tn=768, name="proj_mla")
            gc, gr = _gates(u, w["wc"], w["wr"], w["bc"], w["br"])
            ya = _mlstm(p1, gc, gr, p2, mlstm_out_norm[l][None])
            qh, kh, vh = _mla_prep(p3, mla_q_norm[l][None], mla_kv_norm[l][None], w["wq"], w["wkv"], cs)
            yb = _flash(qh, kh, vh)
            yc = _lru(p2, lru_conv_w[l], lru_conv_b[l][None], w["wax"], lru_b_a[l][None],
                      lru_b_x[l][None], lru_lambda[l][None])
            xb = _merge(xb, u, ya, yb, yc, w["wg"], w["wb"], w["wo"])
            last = l == depth - 1
            xb = _ffn(xb, ffn2_norm[l][None], ffn2_w_gate[l].astype(BF16), ffn2_w_up[l].astype(BF16),
                      ffn2_w_down[l].astype(BF16), final_norm[None] if last else ffn2_norm[l][None],
                      mode="final" if last else "plain")
        outs.append(xb)
    return jnp.stack(outs, axis=0)
```

```python
import functools

import jax
import jax.numpy as jnp
from jax import lax
from jax.experimental import pallas as pl
from jax.experimental.pallas import tpu as pltpu

F32 = jnp.float32
BF16 = jnp.bfloat16

D_MODEL = 2048
D_FF = 5632
NORM_EPS = 1e-6
NEG = -1e30
LOG2_E = 1.4426950408889634

A_HEADS = 4
A_DQK = 128
A_DV = 256
A_WIDTH = A_HEADS * A_DV

B_HEADS = 8
B_Q_LORA = 384
B_KV_LORA = 256
B_NOPE = 128
B_ROPE = 64
B_DV = 128
B_DK_PAD = 256
ROPE_BASE = 10000.0

C_WIDTH = 1024
C_BLOCKS = 8
C_BLOCK_DIM = C_WIDTH // C_BLOCKS
C_CONV = 4
C_POW = 8.0

N_BRANCH = 3

_OFF_Q = 0
_OFF_K = 512
_OFF_V = 1024
_OFF_I = 2048
_OFF_O = 2056
_OFF_CQ = 3080
_OFF_CKV = 3464
_OFF_KR = 3720
_OFF_CX = 3784
_OFF_G = 4808

VMEM_LIMIT_BYTES = 56 * 1024 * 1024


def _cparams(*sem):
    return pltpu.CompilerParams(dimension_semantics=sem, vmem_limit_bytes=VMEM_LIMIT_BYTES)


def _sigmoid(x):
    return 1.0 / (1.0 + jnp.exp(-x))


def _softplus(x):
    return jnp.maximum(x, 0.0) + jnp.log1p(jnp.exp(-jnp.abs(x)))


def _rms(x, g):
    return x * lax.rsqrt(jnp.mean(x * x, axis=-1, keepdims=True) + NORM_EPS) * g


def _dot(a, b, **kw):
    return jnp.dot(a, b, preferred_element_type=F32, **kw)


def _dot_nt(a, b):
    return lax.dot_general(a, b, (((1,), (1,)), ((), ())), preferred_element_type=F32)


def _dot_tn(a, b):
    return lax.dot_general(a, b, (((0,), (0,)), ((), ())), preferred_element_type=F32)


def _ffn_body(x_ref, g_ref, wg_ref, wu_ref, wd_ref, g2_ref, *rest, n_ff, mode):
    if mode == "mix":
        o_ref, u_ref, xn_ref = rest
    else:
        o_ref, xn_ref = rest
    j = pl.program_id(1)

    @pl.when(j == 0)
    def _():
        xn_ref[...] = _rms(x_ref[...], g_ref[...]).astype(BF16)
        o_ref[...] = jnp.zeros_like(o_ref)

    xn = xn_ref[...]
    h1 = _dot(xn, wg_ref[...])
    h2 = _dot(xn, wu_ref[...])
    h = (h1 * _sigmoid(h1) * h2).astype(BF16)
    o_ref[...] += _dot(h, wd_ref[...])

    @pl.when(j == n_ff - 1)
    def _():
        y = x_ref[...] + 0.5 * o_ref[...]
        if mode == "final":
            o_ref[...] = _rms(y, g2_ref[...])
        else:
            o_ref[...] = y
        if mode == "mix":
            u_ref[...] = _rms(y, g2_ref[...]).astype(BF16)


def _ffn(x, g, wg, wu, wd, g2, *, mode, tm=512, tf=512):
    s, d = x.shape
    n_ff = wg.shape[1] // tf
    out_shape = [jax.ShapeDtypeStruct((s, d), F32)]
    out_specs = [pl.BlockSpec((tm, d), lambda i, j: (i, 0))]
    if mode == "mix":
        out_shape.append(jax.ShapeDtypeStruct((s, d), BF16))
        out_specs.append(pl.BlockSpec((tm, d), lambda i, j: (i, 0)))
    res = pl.pallas_call(
        functools.partial(_ffn_body, n_ff=n_ff, mode=mode),
        grid=(s // tm, n_ff),
        in_specs=[
            pl.BlockSpec((tm, d), lambda i, j: (i, 0)),
            pl.BlockSpec((1, d), lambda i, j: (0, 0)),
            pl.BlockSpec((d, tf), lambda i, j: (0, j)),
            pl.BlockSpec((d, tf), lambda i, j: (0, j)),
            pl.BlockSpec((tf, d), lambda i, j: (j, 0)),
            pl.BlockSpec((1, d), lambda i, j: (0, 0)),
        ],
        out_specs=out_specs,
        out_shape=out_shape,
        scratch_shapes=[pltpu.VMEM((tm, d), BF16)],
        compiler_params=_cparams("parallel", "arbitrary"),
        name="ffn_" + mode,
    )(x, g, wg, wu, wd, g2)
    return res if mode == "mix" else res[0]


def _mm_body(a_ref, w_ref, o_ref):
    o_ref[...] = _dot(a_ref[...], w_ref[...]).astype(o_ref.dtype)


def _matmul(a, w, out_dtype, *, tm=1024, tn=512, name="proj"):
    s, k = a.shape
    n = w.shape[1]
    tn = min(tn, n)
    return pl.pallas_call(
        _mm_body,
        grid=(s // tm, n // tn),
        in_specs=[pl.BlockSpec((tm, k), lambda i, j: (i, 0)),
                  pl.BlockSpec((k, tn), lambda i, j: (0, j))],
        out_specs=pl.BlockSpec((tm, tn), lambda i, j: (i, j)),
        out_shape=jax.ShapeDtypeStruct((s, n), out_dtype),
        compiler_params=_cparams("parallel", "arbitrary"),
        name=name,
    )(a, w)


def _log_sigmoid(x):
    return jnp.minimum(x, 0.0) - jnp.log1p(jnp.exp(-jnp.abs(x)))


def _gates_body(u_ref, wc_ref, wr_ref, bc_ref, br_ref, gc_ref, gr_ref):
    u = u_ref[...]
    gc = _dot(u, wc_ref[...]) + bc_ref[...]
    gr = _dot_nt(wr_ref[...], u) + br_ref[...]
    lane = lax.broadcasted_iota(jnp.int32, gc.shape, 1)
    row = lax.broadcasted_iota(jnp.int32, gr.shape, 0)
    gc_ref[...] = jnp.where(lane >= A_HEADS, _log_sigmoid(gc), gc)
    gr_ref[...] = jnp.where(row >= A_HEADS, _log_sigmoid(gr), gr)


def _gates(u, wc, wr, bc, br, *, tm=1024):
    s, d = u.shape
    return pl.pallas_call(
        _gates_body,
        grid=(s // tm,),
        in_specs=[pl.BlockSpec((tm, d), lambda i: (i, 0)),
                  pl.BlockSpec((d, 128), lambda i: (0, 0)),
                  pl.BlockSpec((16, d), lambda i: (0, 0)),
                  pl.BlockSpec((1, 128), lambda i: (0, 0)),
                  pl.BlockSpec((16, 1), lambda i: (0, 0))],
        out_specs=[pl.BlockSpec((tm, 128), lambda i: (i, 0)),
                   pl.BlockSpec((16, tm), lambda i: (0, i))],
        out_shape=[jax.ShapeDtypeStruct((s, 128), F32),
                   jax.ShapeDtypeStruct((16, s), F32)],
        compiler_params=_cparams("parallel"),
        name="mlstm_gates",
    )(u, wc, wr, bc, br)


def _mlstm_body(q_ref, k_ref, v_ref, gc_ref, gr_ref, o_ref, nw_ref, y_ref,
                c_ref, n_ref, m_ref, *, chunk):
    L = chunk

    @pl.when(pl.program_id(0) == 0)
    def _():
        c_ref[...] = jnp.zeros_like(c_ref)
        n_ref[...] = jnp.zeros_like(n_ref)
        m_ref[...] = jnp.full_like(m_ref, NEG)

    row = lax.broadcasted_iota(jnp.int32, (L, L), 0)
    col = lax.broadcasted_iota(jnp.int32, (L, L), 1)
    causal = col <= row
    tril = causal.astype(F32)
    triu = (row <= col).astype(F32)
    gc = gc_ref[...]
    gr = gr_ref[...]
    cum_c = _dot(tril, gc, precision=lax.Precision.HIGHEST)
    cum_r = _dot(gr, triu, precision=lax.Precision.HIGHEST)

    for h in range(A_HEADS):
        bcol = cum_c[:, A_HEADS + h:A_HEADS + h + 1]
        brow = cum_r[A_HEADS + h:A_HEADS + h + 1, :]
        icol = gc[:, h:h + 1]
        irow = gr[h:h + 1, :]
        btot = brow[:, L - 1:L]
        m_prev = m_ref[h][0:1, 0:1]

        d_intra = jnp.where(causal, bcol - brow + irow, NEG)
        d_inter = bcol + m_prev
        m_t = jnp.maximum(jnp.max(d_intra, axis=1, keepdims=True), d_inter)
        w_intra = jnp.exp(d_intra - m_t)
        w_inter = jnp.exp(d_inter - m_t)

        qh = q_ref[:, h * A_DQK:(h + 1) * A_DQK]
        kh = k_ref[:, h * A_DQK:(h + 1) * A_DQK]
        vh = v_ref[:, h * A_DV:(h + 1) * A_DV]
        c_prev = c_ref[h]
        n_prev = n_ref[h][0:1, :]

        sc = _dot_nt(qh, kh) * w_intra
        num = _dot(sc.astype(BF16), vh) + w_inter * _dot(qh, c_prev.astype(BF16))
        qn = jnp.sum(qh.astype(F32) * n_prev, axis=1, keepdims=True)
        den = jnp.sum(sc, axis=1, keepdims=True) + w_inter * qn
        h_out = num / jnp.maximum(jnp.abs(den), jnp.exp(-m_t))

        d_state_r = btot - brow + irow
        m_new = jnp.maximum(btot + m_prev, jnp.max(d_state_r, axis=1, keepdims=True))
        w_state = jnp.exp(btot - bcol + icol - m_new)
        w_prev = jnp.exp(btot + m_prev - m_new)
        kw = kh.astype(F32) * w_state
        c_ref[h] = w_prev * c_prev + _dot_tn(kw.astype(BF16), vh)
        n_new = w_prev * n_prev + jnp.sum(kw, axis=0, keepdims=True)
        n_ref[h] = jnp.broadcast_to(n_new, n_ref.shape[1:])
        m_ref[h] = jnp.broadcast_to(m_new, m_ref.shape[1:])

        sl = slice(h * A_DV, (h + 1) * A_DV)
        hn = _rms(h_out, nw_ref[:, sl])
        y_ref[:, sl] = (_sigmoid(o_ref[:, sl]) * hn).astype(BF16)


def _mlstm(p1, gc, gr, p2, nw, *, chunk=256):
    s = p1.shape[0]
    L = chunk
    return pl.pallas_call(
        functools.partial(_mlstm_body, chunk=L),
        grid=(s // L,),
        in_specs=[pl.BlockSpec((L, 512), lambda c: (c, 0)),
                  pl.BlockSpec((L, 512), lambda c: (c, 1)),
                  pl.BlockSpec((L, 1024), lambda c: (c, 1)),
                  pl.BlockSpec((L, 128), lambda c: (c, 0)),
                  pl.BlockSpec((16, L), lambda c: (0, c)),
                  pl.BlockSpec((L, 1024), lambda c: (c, 0)),
                  pl.BlockSpec((1, A_WIDTH), lambda c: (0, 0))],
        out_specs=pl.BlockSpec((L, A_WIDTH), lambda c: (c, 0)),
        out_shape=jax.ShapeDtypeStruct((s, A_WIDTH), BF16),
        scratch_shapes=[pltpu.VMEM((A_HEADS, A_DQK, A_DV), F32),
                        pltpu.VMEM((A_HEADS, 8, A_DQK), F32),
                        pltpu.VMEM((A_HEADS, 8, 128), F32)],
        compiler_params=_cparams("arbitrary"),
        name="mlstm",
    )(p1, p1, p1, gc, gr, p2, nw)


FLASH_BLOCK = 512


def _mla_prep_body(p3_ref, qn_ref, kvn_ref, wqt_ref, wk_ref, wvt_ref, cs_ref, cst_ref,
                   qt_out, k_out, vt_out):
    p3 = p3_ref[...]
    cq = _rms(p3[:, 0:B_Q_LORA], qn_ref[...]).astype(BF16)
    ckv = _rms(p3[:, 512:768], kvn_ref[...]).astype(BF16)
    cs = cs_ref[...]
    cst = cst_ref[...]
    low = lax.broadcasted_iota(jnp.int32, cs.shape, 1) < B_ROPE

    tt = p3[:, 384:512] * cs
    kr = jnp.where(low, tt + pltpu.roll(tt, B_ROPE, axis=1), 0.0).astype(BF16)
    kn = _dot(ckv, wk_ref[...])
    qt = _dot_nt(wqt_ref[...], cq)
    vt = _dot_nt(wvt_ref[...], ckv)
    zeros = jnp.zeros((B_ROPE, cst.shape[1]), BF16)
    for h in range(B_HEADS):
        k_out[h, :, 0:128] = kn[:, h * 128:(h + 1) * 128].astype(BF16)
        k_out[h, :, 128:256] = kr
        qt_out[h, 0:128, :] = qt[h * 128:(h + 1) * 128, :].astype(BF16)
        tq = qt[1024 + h * 128:1024 + (h + 1) * 128, :] * cst
        qt_out[h, 128:192, :] = (tq[0:B_ROPE, :] + tq[B_ROPE:2 * B_ROPE, :]).astype(BF16)
        qt_out[h, 192:256, :] = zeros
        vt_out[h, 0] = vt[h * 128:(h + 1) * 128, :].astype(BF16)


def _mla_prep(p3, qn, kvn, wqt, wk, wvt, cs, cst):
    s = p3.shape[0]
    tm = FLASH_BLOCK
    return pl.pallas_call(
        _mla_prep_body,
        grid=(s // tm,),
        in_specs=[pl.BlockSpec((tm, 768), lambda i: (i, 0)),
                  pl.BlockSpec((1, B_Q_LORA), lambda i: (0, 0)),
                  pl.BlockSpec((1, B_KV_LORA), lambda i: (0, 0)),
                  pl.BlockSpec((2048, B_Q_LORA), lambda i: (0, 0)),
                  pl.BlockSpec((B_KV_LORA, 1024), lambda i: (0, 0)),
                  pl.BlockSpec((1024, B_KV_LORA), lambda i: (0, 0)),
                  pl.BlockSpec((tm, 128), lambda i: (i, 0)),
                  pl.BlockSpec((128, tm), lambda i: (0, i))],
        out_specs=[pl.BlockSpec((B_HEADS, B_DK_PAD, tm), lambda i: (0, 0, i)),
                   pl.BlockSpec((B_HEADS, tm, B_DK_PAD), lambda i: (0, i, 0)),
                   pl.BlockSpec((B_HEADS, 1, B_DV, tm), lambda i: (0, i, 0, 0))],
        out_shape=[jax.ShapeDtypeStruct((B_HEADS, B_DK_PAD, s), BF16),
                   jax.ShapeDtypeStruct((B_HEADS, s, B_DK_PAD), BF16),
                   jax.ShapeDtypeStruct((B_HEADS, s // tm, B_DV, tm), BF16)],
        compiler_params=_cparams("parallel"),
        name="mla_prep",
    )(p3, qn, kvn, wqt, wk, wvt, cs, cst)


def _flash_body(qt_ref, k_ref, vt_ref, o_ref, sa_ref, sb_ref, mxa_ref, mxb_ref, acc_ref, m_ref, l_ref,
                *, blk):
    qi = pl.program_id(1)
    qt = qt_ref[...]
    slot_a = (sa_ref, mxa_ref)
    slot_b = (sb_ref, mxb_ref)

    def scores(kb, slot, masked=False):
        s_ref, mx_ref = slot
        off = pl.multiple_of(kb * blk, blk)
        s = _dot(k_ref[pl.ds(off, blk), :], qt)
        if masked:
            key = lax.broadcasted_iota(jnp.int32, (blk, blk), 0)
            qry = lax.broadcasted_iota(jnp.int32, (blk, blk), 1)
            s = jnp.where(key <= qry, s, NEG)
        s_ref[...] = s
        mx_ref[...] = jnp.broadcast_to(jnp.max(s, axis=0, keepdims=True), mx_ref.shape)

    def softmax_pv(kb, slot):
        s_ref, mx_ref = slot
        m_prev = m_ref[0:1, :]
        m_new = jnp.maximum(m_prev, mx_ref[0:1, :])
        alpha = jnp.exp2(m_prev - m_new)
        p = jnp.exp2(s_ref[...] - m_new)
        l_ref[...] = jnp.broadcast_to(alpha * l_ref[0:1, :] + jnp.sum(p, axis=0, keepdims=True), l_ref.shape)
        m_ref[...] = jnp.broadcast_to(m_new, m_ref.shape)
        acc_ref[...] = alpha * acc_ref[...] + _dot(vt_ref[kb], p.astype(BF16))

    m_ref[...] = jnp.full_like(m_ref, NEG)
    l_ref[...] = jnp.zeros_like(l_ref)
    acc_ref[...] = jnp.zeros_like(acc_ref)
    scores(qi, slot_a, masked=True)

    def pair(j, kb_cur):
        scores(j, slot_b)
        softmax_pv(kb_cur, slot_a)
        scores(j + 1, slot_a)
        softmax_pv(j, slot_b)
        return j + 1

    def quad(jj, kb_cur):
        return pair(4 * jj + 2, pair(4 * jj, kb_cur))

    n_quad = qi // 4
    kb_cur = lax.fori_loop(0, n_quad, quad, qi)
    rem = qi - 4 * n_quad

    @pl.when(rem == 0)
    def _():
        softmax_pv(kb_cur, slot_a)

    @pl.when(rem == 1)
    def _():
        scores(qi - 1, slot_b)
        softmax_pv(kb_cur, slot_a)
        softmax_pv(qi - 1, slot_b)

    @pl.when(rem == 2)
    def _():
        softmax_pv(pair(qi - 2, kb_cur), slot_a)

    @pl.when(rem == 3)
    def _():
        kb = pair(qi - 3, kb_cur)
        scores(qi - 1, slot_b)
        softmax_pv(kb, slot_a)
        softmax_pv(qi - 1, slot_b)

    o = acc_ref[...] / l_ref[0:1, :]
    o_ref[...] = o.T.astype(BF16)


def _flash(qt, k, vt):
    nh, dk, s = qt.shape
    blk = FLASH_BLOCK
    return pl.pallas_call(
        functools.partial(_flash_body, blk=blk),
        grid=(nh, s // blk),
        in_specs=[pl.BlockSpec((None, dk, blk), lambda h, i: (h, 0, i)),
                  pl.BlockSpec((None, s, dk), lambda h, i: (h, 0, 0)),
                  pl.BlockSpec((None, s // blk, B_DV, blk), lambda h, i: (h, 0, 0, 0))],
        out_specs=pl.BlockSpec((blk, B_DV), lambda h, i: (i, h)),
        out_shape=jax.ShapeDtypeStruct((s, nh * B_DV), BF16),
        scratch_shapes=[pltpu.VMEM((blk, blk), F32),
                        pltpu.VMEM((blk, blk), F32),
                        pltpu.VMEM((8, blk), F32),
                        pltpu.VMEM((8, blk), F32),
                        pltpu.VMEM((B_DV, blk), F32),
                        pltpu.VMEM((8, blk), F32),
                        pltpu.VMEM((8, blk), F32)],
        compiler_params=_cparams("parallel", "arbitrary"),
        name="mla_flash",
    )(qt, k, vt)


def _lru_body(cx_ref, cw_ref, cb_ref, wax_ref, ba_ref, bx_ref, lam_ref, y_ref,
              xs_ref, h_ref, *, tb):
    T = tb

    @pl.when(pl.program_id(0) == 0)
    def _():
        xs_ref[0:8, :] = jnp.zeros((8, C_WIDTH), F32)
        h_ref[...] = jnp.zeros_like(h_ref)

    x = cx_ref[...]
    xs_ref[8:8 + T, :] = x
    w = cw_ref[...]
    xc = (w[3:4, :] * x + w[2:3, :] * xs_ref[7:7 + T, :] + w[1:2, :] * xs_ref[6:6 + T, :]
          + w[0:1, :] * xs_ref[5:5 + T, :] + cb_ref[...])
    xs_ref[0:8, :] = x[T - 8:T, :]

    ra, rx = [], []
    for b in range(C_BLOCKS):
        xb = xc[:, b * C_BLOCK_DIM:(b + 1) * C_BLOCK_DIM].astype(BF16)
        g = _dot(xb, wax_ref[b])
        ra.append(g[:, :C_BLOCK_DIM])
        rx.append(g[:, C_BLOCK_DIM:])
    r = _sigmoid(jnp.concatenate(ra, axis=1) + ba_ref[...])
    gi = _sigmoid(jnp.concatenate(rx, axis=1) + bx_ref[...])
    log_a = -C_POW * r * _softplus(-lam_ref[...])
    a = jnp.exp(log_a)
    u = jnp.sqrt(-jnp.tanh(log_a) * (a * a + 1.0)) * (gi * xc)

    rowi = lax.broadcasted_iota(jnp.int32, (T, C_WIDTH), 0)
    d = 1
    while d < T:
        keep = rowi >= d
        a_s = jnp.where(keep, pltpu.roll(a, d, axis=0), 1.0)
        u_s = jnp.where(keep, pltpu.roll(u, d, axis=0), 0.0)
        u = a * u_s + u
        a = a * a_s
        d *= 2
    hh = a * h_ref[0:1, :] + u
    h_ref[...] = jnp.broadcast_to(hh[T - 1:T, :], h_ref.shape)
    y_ref[...] = hh.astype(BF16)


def _lru(p2, cw, cb, wax, ba, bx, lam, *, tb=256):
    s = p2.shape[0]
    vec = pl.BlockSpec((1, C_WIDTH), lambda c: (0, 0))
    return pl.pallas_call(
        functools.partial(_lru_body, tb=tb),
        grid=(s // tb,),
        in_specs=[pl.BlockSpec((tb, C_WIDTH), lambda c: (c, 1)),
                  pl.BlockSpec((C_CONV, C_WIDTH), lambda c: (0, 0)),
                  vec,
                  pl.BlockSpec((C_BLOCKS, C_BLOCK_DIM, 2 * C_BLOCK_DIM), lambda c: (0, 0, 0)),
                  vec, vec, vec],
        out_specs=pl.BlockSpec((tb, C_WIDTH), lambda c: (c, 0)),
        out_shape=jax.ShapeDtypeStruct((s, C_WIDTH), BF16),
        scratch_shapes=[pltpu.VMEM((tb + 8, C_WIDTH), F32),
                        pltpu.VMEM((8, C_WIDTH), F32)],
        compiler_params=_cparams("arbitrary"),
        name="rglru",
    )(p2, cw, cb, wax, ba, bx, lam)


def _merge_body(x_ref, u_ref, ya_ref, yb_ref, yc_ref, wg_ref, wb_ref, wo_ref, o_ref, *, n_tiles):
    n = pl.program_id(1)

    @pl.when(n == 0)
    def _():
        o_ref[...] = jnp.zeros_like(o_ref)

    u = u_ref[...]
    z = None
    for j, y_ref in enumerate((ya_ref, yb_ref, yc_ref)):
        t = _sigmoid(_dot(u, wg_ref[j])) * _dot(y_ref[...], wb_ref[j])
        z = t if z is None else z + t
    o_ref[...] += _dot(z.astype(BF16), wo_ref[...])

    @pl.when(n == n_tiles - 1)
    def _():
        o_ref[...] = x_ref[...] + o_ref[...]


def _merge(x, u, ya, yb, yc, wg, wb, wo, *, tm=512, tn=512):
    s, d = x.shape
    n_tiles = d // tn
    br = pl.BlockSpec((tm, 1024), lambda i, n: (i, 0))
    return pl.pallas_call(
        functools.partial(_merge_body, n_tiles=n_tiles),
        grid=(s // tm, n_tiles),
        in_specs=[pl.BlockSpec((tm, d), lambda i, n: (i, 0)),
                  pl.BlockSpec((tm, d), lambda i, n: (i, 0)),
                  br, br, br,
                  pl.BlockSpec((N_BRANCH, d, tn), lambda i, n: (0, 0, n)),
                  pl.BlockSpec((N_BRANCH, 1024, tn), lambda i, n: (0, 0, n)),
                  pl.BlockSpec((tn, d), lambda i, n: (n, 0))],
        out_specs=pl.BlockSpec((tm, d), lambda i, n: (i, 0)),
        out_shape=jax.ShapeDtypeStruct((s, d), F32),
        compiler_params=_cparams("parallel", "arbitrary"),
        name="merge",
    )(x, u, ya, yb, yc, wg, wb, wo)


def _rot_cols(w):
    half = B_ROPE // 2
    return jnp.concatenate([-w[..., half:], w[..., :half]], axis=-1)


def _prep_layer(w_in, gate_bias, w_uq, w_ukv, lru_w_a, lru_w_x, w_branch, w_out):
    d = w_in.shape[0]
    w1 = jnp.concatenate([w_in[:, _OFF_Q:_OFF_K] * (A_DQK ** -0.5), w_in[:, _OFF_K:_OFF_I]], axis=1).astype(BF16)
    w2 = jnp.concatenate([w_in[:, _OFF_O:_OFF_CQ], w_in[:, _OFF_CX:_OFF_G]], axis=1).astype(BF16)
    w_kr = w_in[:, _OFF_KR:_OFF_CX]
    w3 = jnp.concatenate([w_in[:, _OFF_CQ:_OFF_CKV], w_kr, _rot_cols(w_kr),
                          w_in[:, _OFF_CKV:_OFF_KR]], axis=1).astype(BF16)
    w_if = w_in[:, _OFF_I:_OFF_O]
    wc = jnp.concatenate([w_if, jnp.zeros((d, 120), F32)], axis=1).astype(BF16)
    wr = jnp.concatenate([w_if.T, jnp.zeros((8, d), F32)], axis=0).astype(BF16)
    bc = jnp.concatenate([gate_bias, jnp.zeros((120,), F32)])[None, :]
    br = jnp.concatenate([gate_bias, jnp.zeros((8,), F32)])[:, None]

    scale = (B_NOPE + B_ROPE) ** -0.5 * LOG2_E
    wq3 = w_uq.reshape(B_Q_LORA, B_HEADS, B_NOPE + B_ROPE)
    q_nope = wq3[:, :, :B_NOPE].reshape(B_Q_LORA, B_HEADS * B_NOPE)
    q_rope = wq3[:, :, B_NOPE:]
    q_rr = jnp.concatenate([q_rope, _rot_cols(q_rope)], axis=-1).reshape(B_Q_LORA, B_HEADS * 128)
    wqt = (jnp.concatenate([q_nope, q_rr], axis=1) * scale).T.astype(BF16)
    wkv3 = w_ukv.reshape(B_KV_LORA, B_HEADS, B_NOPE + B_DV)
    wk = wkv3[:, :, :B_NOPE].reshape(B_KV_LORA, -1).astype(BF16)
    wvt = wkv3[:, :, B_NOPE:].reshape(B_KV_LORA, -1).T.astype(BF16)
    wax = jnp.concatenate([lru_w_a, lru_w_x], axis=-1).astype(BF16)
    wg = w_in[:, _OFF_G:].reshape(d, N_BRANCH, D_MODEL).transpose(1, 0, 2).astype(BF16)
    return dict(w1=w1, w2=w2, w3=w3, wc=wc, wr=wr, bc=bc, br=br, wqt=wqt, wk=wk, wvt=wvt, wax=wax,
                wg=wg, wb=w_branch.astype(BF16), wo=w_out.astype(BF16))


def kernel(x, ffn1_norm, ffn1_w_gate, ffn1_w_up, ffn1_w_down, mix_norm, w_in, mlstm_gate_bias, mlstm_out_norm, mla_q_norm, mla_w_uq, mla_kv_norm, mla_w_ukv, lru_conv_w, lru_conv_b, lru_w_a, lru_b_a, lru_w_x, lru_b_x, lru_lambda, w_branch, w_out, ffn2_norm, ffn2_w_gate, ffn2_w_up, ffn2_w_down, final_norm):
    b, s, d = x.shape
    depth = w_in.shape[0]
    pos = jnp.arange(s, dtype=F32)
    inv_freq = jnp.power(ROPE_BASE, -jnp.arange(0, B_ROPE, 2, dtype=F32) / B_ROPE)
    ang = pos[:, None] * inv_freq[None, :]
    cos, sin = jnp.cos(ang), jnp.sin(ang)
    cs = jnp.concatenate([cos, cos, sin, sin], axis=1)
    cst = cs.T

    outs = []
    for bi in range(b):
        xb = x[bi]
        for l in range(depth):
            w = _prep_layer(w_in[l], mlstm_gate_bias[l], mla_w_uq[l], mla_w_ukv[l],
                            lru_w_a[l], lru_w_x[l], w_branch[l], w_out[l])
            xb, u = _ffn(xb, ffn1_norm[l][None], ffn1_w_gate[l].astype(BF16), ffn1_w_up[l].astype(BF16),
                         ffn1_w_down[l].astype(BF16), mix_norm[l][None], mode="mix")
            p1 = _matmul(u, w["w1"], BF16, name="proj_qkv")
            p2 = _matmul(u, w["w2"], F32, name="proj_o_cx")
            p3 = _matmul(u, w["w3"], F32, tn=768, name="proj_mla")
            gc, gr = _gates(u, w["wc"], w["wr"], w["bc"], w["br"])
            ya = _mlstm(p1, gc, gr, p2, mlstm_out_norm[l][None])
            qt, kh, vt = _mla_prep(p3, mla_q_norm[l][None], mla_kv_norm[l][None], w["wqt"], w["wk"],
                                   w["wvt"], cs, cst)
            yb = _flash(qt, kh, vt)
            yc = _lru(p2, lru_conv_w[l], lru_conv_b[l][None], w["wax"], lru_b_a[l][None],
                      lru_b_x[l][None], lru_lambda[l][None])
            xb = _merge(xb, u, ya, yb, yc, w["wg"], w["wb"], w["wo"])
            last = l == depth - 1
            xb = _ffn(xb, ffn2_norm[l][None], ffn2_w_gate[l].astype(BF16), ffn2_w_up[l].astype(BF16),
                      ffn2_w_down[l].astype(BF16), final_norm[None] if last else ffn2_norm[l][None],
                      mode="final" if last else "plain")
        outs.append(xb)
    return jnp.stack(outs, axis=0)
```

```python
import functools

import jax
import jax.numpy as jnp
from jax import lax
from jax.experimental import pallas as pl
from jax.experimental.pallas import tpu as pltpu

F32 = jnp.float32
BF16 = jnp.bfloat16

D_MODEL = 2048
D_FF = 5632
NORM_EPS = 1e-6
NEG = -1e30
LOG2_E = 1.4426950408889634

A_HEADS = 4
A_DQK = 128
A_DV = 256
A_WIDTH = A_HEADS * A_DV

B_HEADS = 8
B_Q_LORA = 384
B_KV_LORA = 256
B_NOPE = 128
B_ROPE = 64
B_DV = 128
B_DK_PAD = 256
ROPE_BASE = 10000.0

C_WIDTH = 1024
C_BLOCKS = 8
C_BLOCK_DIM = C_WIDTH // C_BLOCKS
C_CONV = 4
C_POW = 8.0

N_BRANCH = 3

_OFF_Q = 0
_OFF_K = 512
_OFF_V = 1024
_OFF_I = 2048
_OFF_O = 2056
_OFF_CQ = 3080
_OFF_CKV = 3464
_OFF_KR = 3720
_OFF_CX = 3784
_OFF_G = 4808

VMEM_LIMIT_BYTES = 56 * 1024 * 1024


def _cparams(*sem):
    return pltpu.CompilerParams(dimension_semantics=sem, vmem_limit_bytes=VMEM_LIMIT_BYTES)


def _sigmoid(x):
    return 1.0 / (1.0 + jnp.exp(-x))


def _softplus(x):
    return jnp.maximum(x, 0.0) + jnp.log1p(jnp.exp(-jnp.abs(x)))


def _rms(x, g):
    return x * lax.rsqrt(jnp.mean(x * x, axis=-1, keepdims=True) + NORM_EPS) * g


def _dot(a, b, **kw):
    return jnp.dot(a, b, preferred_element_type=F32, **kw)


def _dot_nt(a, b):
    return lax.dot_general(a, b, (((1,), (1,)), ((), ())), preferred_element_type=F32)


def _dot_tn(a, b):
    return lax.dot_general(a, b, (((0,), (0,)), ((), ())), preferred_element_type=F32)


def _ffn_body(x_ref, g_ref, wg_ref, wu_ref, wd_ref, g2_ref, *rest, n_ff, mode):
    if mode == "mix":
        o_ref, u_ref, xn_ref = rest
    else:
        o_ref, xn_ref = rest
    j = pl.program_id(1)

    @pl.when(j == 0)
    def _():
        xn_ref[...] = _rms(x_ref[...], g_ref[...]).astype(BF16)
        o_ref[...] = jnp.zeros_like(o_ref)

    xn = xn_ref[...]
    h1 = _dot(xn, wg_ref[...])
    h2 = _dot(xn, wu_ref[...])
    h = (h1 * _sigmoid(h1) * h2).astype(BF16)
    o_ref[...] += _dot(h, wd_ref[...])

    @pl.when(j == n_ff - 1)
    def _():
        y = x_ref[...] + 0.5 * o_ref[...]
        if mode == "final":
            o_ref[...] = _rms(y, g2_ref[...])
        else:
            o_ref[...] = y
        if mode == "mix":
            u_ref[...] = _rms(y, g2_ref[...]).astype(BF16)


def _cast_body(x_ref, o_ref):
    o_ref[...] = x_ref[...].astype(o_ref.dtype)


def _cast_bf16(w, *, rows):
    shape = w.shape
    w2 = w.reshape(-1, shape[-1])
    r, c = w2.shape
    out = pl.pallas_call(
        _cast_body,
        grid=(r // rows,),
        in_specs=[pl.BlockSpec((rows, c), lambda i: (i, 0))],
        out_specs=pl.BlockSpec((rows, c), lambda i: (i, 0)),
        out_shape=jax.ShapeDtypeStruct((r, c), BF16),
        compiler_params=_cparams("parallel"),
        name="cast_bf16",
    )(w2)
    return out.reshape(shape)


def _ffn(x, g, wg, wu, wd, g2, *, layer, mode, tm=512, tf=512):
    s, d = x.shape
    n_ff = wg.shape[2] // tf
    out_shape = [jax.ShapeDtypeStruct((s, d), F32)]
    out_specs = [pl.BlockSpec((tm, d), lambda i, j: (i, 0))]
    if mode == "mix":
        out_shape.append(jax.ShapeDtypeStruct((s, d), BF16))
        out_specs.append(pl.BlockSpec((tm, d), lambda i, j: (i, 0)))
    res = pl.pallas_call(
        functools.partial(_ffn_body, n_ff=n_ff, mode=mode),
        grid=(s // tm, n_ff),
        in_specs=[
            pl.BlockSpec((tm, d), lambda i, j: (i, 0)),
            pl.BlockSpec((1, d), lambda i, j: (0, 0)),
            pl.BlockSpec((None, d, tf), lambda i, j: (layer, 0, j)),
            pl.BlockSpec((None, d, tf), lambda i, j: (layer, 0, j)),
            pl.BlockSpec((None, tf, d), lambda i, j: (layer, j, 0)),
            pl.BlockSpec((1, d), lambda i, j: (0, 0)),
        ],
        out_specs=out_specs,
        out_shape=out_shape,
        scratch_shapes=[pltpu.VMEM((tm, d), BF16)],
        compiler_params=_cparams("parallel", "arbitrary"),
        name="ffn_" + mode,
    )(x, g, wg, wu, wd, g2)
    return res if mode == "mix" else res[0]


def _mm_body(a_ref, w_ref, o_ref):
    o_ref[...] = _dot(a_ref[...], w_ref[...]).astype(o_ref.dtype)


def _matmul(a, w, out_dtype, *, tm=1024, tn=512, name="proj"):
    s, k = a.shape
    n = w.shape[1]
    tn = min(tn, n)
    return pl.pallas_call(
        _mm_body,
        grid=(s // tm, n // tn),
        in_specs=[pl.BlockSpec((tm, k), lambda i, j: (i, 0)),
                  pl.BlockSpec((k, tn), lambda i, j: (0, j))],
        out_specs=pl.BlockSpec((tm, tn), lambda i, j: (i, j)),
        out_shape=jax.ShapeDtypeStruct((s, n), out_dtype),
        compiler_params=_cparams("parallel", "arbitrary"),
        name=name,
    )(a, w)


def _log_sigmoid(x):
    return jnp.minimum(x, 0.0) - jnp.log1p(jnp.exp(-jnp.abs(x)))


def _gates_body(u_ref, wc_ref, wr_ref, bc_ref, br_ref, gc_ref, gr_ref):
    u = u_ref[...]
    gc = _dot(u, wc_ref[...]) + bc_ref[...]
    gr = _dot_nt(wr_ref[...], u) + br_ref[...]
    lane = lax.broadcasted_iota(jnp.int32, gc.shape, 1)
    row = lax.broadcasted_iota(jnp.int32, gr.shape, 0)
    gc_ref[...] = jnp.where(lane >= A_HEADS, _log_sigmoid(gc), gc)
    gr_ref[...] = jnp.where(row >= A_HEADS, _log_sigmoid(gr), gr)


def _gates(u, wc, wr, bc, br, *, tm=1024):
    s, d = u.shape
    return pl.pallas_call(
        _gates_body,
        grid=(s // tm,),
        in_specs=[pl.BlockSpec((tm, d), lambda i: (i, 0)),
                  pl.BlockSpec((d, 128), lambda i: (0, 0)),
                  pl.BlockSpec((16, d), lambda i: (0, 0)),
                  pl.BlockSpec((1, 128), lambda i: (0, 0)),
                  pl.BlockSpec((16, 1), lambda i: (0, 0))],
        out_specs=[pl.BlockSpec((tm, 128), lambda i: (i, 0)),
                   pl.BlockSpec((16, tm), lambda i: (0, i))],
        out_shape=[jax.ShapeDtypeStruct((s, 128), F32),
                   jax.ShapeDtypeStruct((16, s), F32)],
        compiler_params=_cparams("parallel"),
        name="mlstm_gates",
    )(u, wc, wr, bc, br)


def _mlstm_body(q_ref, k_ref, v_ref, gc_ref, gr_ref, o_ref, nw_ref, y_ref,
                c_ref, n_ref, m_ref, *, chunk):
    L = chunk

    @pl.when(pl.program_id(0) == 0)
    def _():
        c_ref[...] = jnp.zeros_like(c_ref)
        n_ref[...] = jnp.zeros_like(n_ref)
        m_ref[...] = jnp.full_like(m_ref, NEG)

    row = lax.broadcasted_iota(jnp.int32, (L, L), 0)
    col = lax.broadcasted_iota(jnp.int32, (L, L), 1)
    causal = col <= row
    tril = causal.astype(F32)
    triu = (row <= col).astype(F32)
    gc = gc_ref[...]
    gr = gr_ref[...]
    cum_c = _dot(tril, gc, precision=lax.Precision.HIGHEST)
    cum_r = _dot(gr, triu, precision=lax.Precision.HIGHEST)

    for h in range(A_HEADS):
        bcol = cum_c[:, A_HEADS + h:A_HEADS + h + 1]
        brow = cum_r[A_HEADS + h:A_HEADS + h + 1, :]
        icol = gc[:, h:h + 1]
        irow = gr[h:h + 1, :]
        btot = brow[:, L - 1:L]
        m_prev = m_ref[h][0:1, 0:1]

        d_intra = jnp.where(causal, bcol - brow + irow, NEG)
        d_inter = bcol + m_prev
        m_t = jnp.maximum(jnp.max(d_intra, axis=1, keepdims=True), d_inter)
        w_intra = jnp.exp(d_intra - m_t)
        w_inter = jnp.exp(d_inter - m_t)

        qh = q_ref[:, h * A_DQK:(h + 1) * A_DQK]
        kh = k_ref[:, h * A_DQK:(h + 1) * A_DQK]
        vh = v_ref[:, h * A_DV:(h + 1) * A_DV]
        c_prev = c_ref[h]
        n_prev = n_ref[h][0:1, :]

        sc = _dot_nt(qh, kh) * w_intra
        num = _dot(sc.astype(BF16), vh) + w_inter * _dot(qh, c_prev.astype(BF16))
        qn = jnp.sum(qh.astype(F32) * n_prev, axis=1, keepdims=True)
        den = jnp.sum(sc, axis=1, keepdims=True) + w_inter * qn
        h_out = num / jnp.maximum(jnp.abs(den), jnp.exp(-m_t))

        d_state_r = btot - brow + irow
        m_new = jnp.maximum(btot + m_prev, jnp.max(d_state_r, axis=1, keepdims=True))
        w_state = jnp.exp(btot - bcol + icol - m_new)
        w_prev = jnp.exp(btot + m_prev - m_new)
        kw = kh.astype(F32) * w_state
        c_ref[h] = w_prev * c_prev + _dot_tn(kw.astype(BF16), vh)
        n_new = w_prev * n_prev + jnp.sum(kw, axis=0, keepdims=True)
        n_ref[h] = jnp.broadcast_to(n_new, n_ref.shape[1:])
        m_ref[h] = jnp.broadcast_to(m_new, m_ref.shape[1:])

        sl = slice(h * A_DV, (h + 1) * A_DV)
        hn = _rms(h_out, nw_ref[:, sl])
        y_ref[:, sl] = (_sigmoid(o_ref[:, sl]) * hn).astype(BF16)


def _mlstm(p1, gc, gr, p2, nw, *, chunk=256):
    s = p1.shape[0]
    L = chunk
    return pl.pallas_call(
        functools.partial(_mlstm_body, chunk=L),
        grid=(s // L,),
        in_specs=[pl.BlockSpec((L, 512), lambda c: (c, 0)),
                  pl.BlockSpec((L, 512), lambda c: (c, 1)),
                  pl.BlockSpec((L, 1024), lambda c: (c, 1)),
                  pl.BlockSpec((L, 128), lambda c: (c, 0)),
                  pl.BlockSpec((16, L), lambda c: (0, c)),
                  pl.BlockSpec((L, 1024), lambda c: (c, 0)),
                  pl.BlockSpec((1, A_WIDTH), lambda c: (0, 0))],
        out_specs=pl.BlockSpec((L, A_WIDTH), lambda c: (c, 0)),
        out_shape=jax.ShapeDtypeStruct((s, A_WIDTH), BF16),
        scratch_shapes=[pltpu.VMEM((A_HEADS, A_DQK, A_DV), F32),
                        pltpu.VMEM((A_HEADS, 8, A_DQK), F32),
                        pltpu.VMEM((A_HEADS, 8, 128), F32)],
        compiler_params=_cparams("arbitrary"),
        name="mlstm",
    )(p1, p1, p1, gc, gr, p2, nw)


FLASH_BLOCK = 512


def _mla_prep_body(p3_ref, qn_ref, kvn_ref, wqt_ref, wk_ref, wvt_ref, cs_ref, cst_ref,
                   qt_out, k_out, vt_out):
    p3 = p3_ref[...]
    cq = _rms(p3[:, 0:B_Q_LORA], qn_ref[...]).astype(BF16)
    ckv = _rms(p3[:, 512:768], kvn_ref[...]).astype(BF16)
    cs = cs_ref[...]
    cst = cst_ref[...]
    low = lax.broadcasted_iota(jnp.int32, cs.shape, 1) < B_ROPE

    tt = p3[:, 384:512] * cs
    kr = jnp.where(low, tt + pltpu.roll(tt, B_ROPE, axis=1), 0.0).astype(BF16)
    kn = _dot(ckv, wk_ref[...])
    qt = _dot_nt(wqt_ref[...], cq)
    vt = _dot_nt(wvt_ref[...], ckv)
    zeros = jnp.zeros((B_ROPE, cst.shape[1]), BF16)
    for h in range(B_HEADS):
        k_out[h, :, 0:128] = kn[:, h * 128:(h + 1) * 128].astype(BF16)
        k_out[h, :, 128:256] = kr
        qt_out[h, 0:128, :] = qt[h * 128:(h + 1) * 128, :].astype(BF16)
        tq = qt[1024 + h * 128:1024 + (h + 1) * 128, :] * cst
        qt_out[h, 128:192, :] = (tq[0:B_ROPE, :] + tq[B_ROPE:2 * B_ROPE, :]).astype(BF16)
        qt_out[h, 192:256, :] = zeros
        vt_out[h, 0] = vt[h * 128:(h + 1) * 128, :].astype(BF16)


def _mla_prep(p3, qn, kvn, wqt, wk, wvt, cs, cst):
    s = p3.shape[0]
    tm = FLASH_BLOCK
    return pl.pallas_call(
        _mla_prep_body,
        grid=(s // tm,),
        in_specs=[pl.BlockSpec((tm, 768), lambda i: (i, 0)),
                  pl.BlockSpec((1, B_Q_LORA), lambda i: (0, 0)),
                  pl.BlockSpec((1, B_KV_LORA), lambda i: (0, 0)),
                  pl.BlockSpec((2048, B_Q_LORA), lambda i: (0, 0)),
                  pl.BlockSpec((B_KV_LORA, 1024), lambda i: (0, 0)),
                  pl.BlockSpec((1024, B_KV_LORA), lambda i: (0, 0)),
                  pl.BlockSpec((tm, 128), lambda i: (i, 0)),
                  pl.BlockSpec((128, tm), lambda i: (0, i))],
        out_specs=[pl.BlockSpec((B_HEADS, B_DK_PAD, tm), lambda i: (0, 0, i)),
                   pl.BlockSpec((B_HEADS, tm, B_DK_PAD), lambda i: (0, i, 0)),
                   pl.BlockSpec((B_HEADS, 1, B_DV, tm), lambda i: (0, i, 0, 0))],
        out_shape=[jax.ShapeDtypeStruct((B_HEADS, B_DK_PAD, s), BF16),
                   jax.ShapeDtypeStruct((B_HEADS, s, B_DK_PAD), BF16),
                   jax.ShapeDtypeStruct((B_HEADS, s // tm, B_DV, tm), BF16)],
        compiler_params=_cparams("parallel"),
        name="mla_prep",
    )(p3, qn, kvn, wqt, wk, wvt, cs, cst)


def _flash_body(qt_ref, k_ref, vt_ref, o_ref, sa_ref, sb_ref, mxa_ref, mxb_ref, acc_ref, m_ref, l_ref,
                *, blk):
    qi = pl.program_id(1)
    qt = qt_ref[...]
    slot_a = (sa_ref, mxa_ref)
    slot_b = (sb_ref, mxb_ref)

    def scores(kb, slot, masked=False):
        s_ref, mx_ref = slot
        off = pl.multiple_of(kb * blk, blk)
        s = _dot(k_ref[pl.ds(off, blk), :], qt)
        if masked:
            key = lax.broadcasted_iota(jnp.int32, (blk, blk), 0)
            qry = lax.broadcasted_iota(jnp.int32, (blk, blk), 1)
            s = jnp.where(key <= qry, s, NEG)
        s_ref[...] = s
        mx_ref[...] = jnp.broadcast_to(jnp.max(s, axis=0, keepdims=True), mx_ref.shape)

    def softmax_pv(kb, slot):
        s_ref, mx_ref = slot
        m_prev = m_ref[0:1, :]
        m_new = jnp.maximum(m_prev, mx_ref[0:1, :])
        alpha = jnp.exp2(m_prev - m_new)
        p = jnp.exp2(s_ref[...] - m_new)
        l_ref[...] = jnp.broadcast_to(alpha * l_ref[0:1, :] + jnp.sum(p, axis=0, keepdims=True), l_ref.shape)
        m_ref[...] = jnp.broadcast_to(m_new, m_ref.shape)
        acc_ref[...] = alpha * acc_ref[...] + _dot(vt_ref[kb], p.astype(BF16))

    m_ref[...] = jnp.full_like(m_ref, NEG)
    l_ref[...] = jnp.zeros_like(l_ref)
    acc_ref[...] = jnp.zeros_like(acc_ref)
    scores(qi, slot_a, masked=True)

    def pair(j, kb_cur):
        scores(j, slot_b)
        softmax_pv(kb_cur, slot_a)
        scores(j + 1, slot_a)
        softmax_pv(j, slot_b)
        return j + 1

    def quad(jj, kb_cur):
        return pair(4 * jj + 2, pair(4 * jj, kb_cur))

    n_quad = qi // 4
    kb_cur = lax.fori_loop(0, n_quad, quad, qi)
    rem = qi - 4 * n_quad

    @pl.when(rem == 0)
    def _():
        softmax_pv(kb_cur, slot_a)

    @pl.when(rem == 1)
    def _():
        scores(qi - 1, slot_b)
        softmax_pv(kb_cur, slot_a)
        softmax_pv(qi - 1, slot_b)

    @pl.when(rem == 2)
    def _():
        softmax_pv(pair(qi - 2, kb_cur), slot_a)

    @pl.when(rem == 3)
    def _():
        kb = pair(qi - 3, kb_cur)
        scores(qi - 1, slot_b)
        softmax_pv(kb, slot_a)
        softmax_pv(qi - 1, slot_b)

    o = acc_ref[...] / l_ref[0:1, :]
    o_ref[...] = o.T.astype(BF16)


def _flash(qt, k, vt):
    nh, dk, s = qt.shape
    blk = FLASH_BLOCK
    return pl.pallas_call(
        functools.partial(_flash_body, blk=blk),
        grid=(nh, s // blk),
        in_specs=[pl.BlockSpec((None, dk, blk), lambda h, i: (h, 0, i)),
                  pl.BlockSpec((None, s, dk), lambda h, i: (h, 0, 0)),
                  pl.BlockSpec((None, s // blk, B_DV, blk), lambda h, i: (h, 0, 0, 0))],
        out_specs=pl.BlockSpec((blk, B_DV), lambda h, i: (i, h)),
        out_shape=jax.ShapeDtypeStruct((s, nh * B_DV), BF16),
        scratch_shapes=[pltpu.VMEM((blk, blk), F32),
                        pltpu.VMEM((blk, blk), F32),
                        pltpu.VMEM((8, blk), F32),
                        pltpu.VMEM((8, blk), F32),
                        pltpu.VMEM((B_DV, blk), F32),
                        pltpu.VMEM((8, blk), F32),
                        pltpu.VMEM((8, blk), F32)],
        compiler_params=_cparams("parallel", "arbitrary"),
        name="mla_flash",
    )(qt, k, vt)


def _lru_body(cx_ref, cw_ref, cb_ref, wax_ref, ba_ref, bx_ref, lam_ref, y_ref,
              xs_ref, h_ref, *, tb):
    T = tb

    @pl.when(pl.program_id(0) == 0)
    def _():
        xs_ref[0:8, :] = jnp.zeros((8, C_WIDTH), F32)
        h_ref[...] = jnp.zeros_like(h_ref)

    x = cx_ref[...]
    xs_ref[8:8 + T, :] = x
    w = cw_ref[...]
    xc = (w[3:4, :] * x + w[2:3, :] * xs_ref[7:7 + T, :] + w[1:2, :] * xs_ref[6:6 + T, :]
          + w[0:1, :] * xs_ref[5:5 + T, :] + cb_ref[...])
    xs_ref[0:8, :] = x[T - 8:T, :]

    ra, rx = [], []
    for b in range(C_BLOCKS):
        xb = xc[:, b * C_BLOCK_DIM:(b + 1) * C_BLOCK_DIM].astype(BF16)
        g = _dot(xb, wax_ref[b])
        ra.append(g[:, :C_BLOCK_DIM])
        rx.append(g[:, C_BLOCK_DIM:])
    r = _sigmoid(jnp.concatenate(ra, axis=1) + ba_ref[...])
    gi = _sigmoid(jnp.concatenate(rx, axis=1) + bx_ref[...])
    log_a = -C_POW * r * _softplus(-lam_ref[...])
    a = jnp.exp(log_a)
    u = jnp.sqrt(-jnp.tanh(log_a) * (a * a + 1.0)) * (gi * xc)

    G = T // 8
    a = a.reshape(G, 8, C_WIDTH)
    u = u.reshape(G, 8, C_WIDTH)
    sub = lax.broadcasted_iota(jnp.int32, (G, 8, C_WIDTH), 1)
    for d in (1, 2, 4):
        keep = sub >= d
        a_s = jnp.where(keep, pltpu.roll(a, d, axis=1), 1.0)
        u_s = jnp.where(keep, pltpu.roll(u, d, axis=1), 0.0)
        u = a * u_s + u
        a = a * a_s
    h_in = h_ref[0:1, :]
    groups = []
    for g in range(G):
        hg = a[g] * h_in + u[g]
        groups.append(hg)
        h_in = hg[7:8, :]
    h_ref[...] = jnp.broadcast_to(h_in, h_ref.shape)
    y_ref[...] = jnp.concatenate(groups, axis=0).astype(BF16)


def _lru(p2, cw, cb, wax, ba, bx, lam, *, tb=256):
    s = p2.shape[0]
    vec = pl.BlockSpec((1, C_WIDTH), lambda c: (0, 0))
    return pl.pallas_call(
        functools.partial(_lru_body, tb=tb),
        grid=(s // tb,),
        in_specs=[pl.BlockSpec((tb, C_WIDTH), lambda c: (c, 1)),
                  pl.BlockSpec((C_CONV, C_WIDTH), lambda c: (0, 0)),
                  vec,
                  pl.BlockSpec((C_BLOCKS, C_BLOCK_DIM, 2 * C_BLOCK_DIM), lambda c: (0, 0, 0)),
                  vec, vec, vec],
        out_specs=pl.BlockSpec((tb, C_WIDTH), lambda c: (c, 0)),
        out_shape=jax.ShapeDtypeStruct((s, C_WIDTH), BF16),
        scratch_shapes=[pltpu.VMEM((tb + 8, C_WIDTH), F32),
                        pltpu.VMEM((8, C_WIDTH), F32)],
        compiler_params=_cparams("arbitrary"),
        name="rglru",
    )(p2, cw, cb, wax, ba, bx, lam)


def _merge_body(x_ref, u_ref, ya_ref, yb_ref, yc_ref, wga_ref, wgb_ref, wgc_ref, wb_ref, wo_ref, o_ref,
                *, n_tiles):
    n = pl.program_id(1)

    @pl.when(n == 0)
    def _():
        o_ref[...] = jnp.zeros_like(o_ref)

    u = u_ref[...]
    z = None
    for j, (y_ref, wg_ref) in enumerate(((ya_ref, wga_ref), (yb_ref, wgb_ref), (yc_ref, wgc_ref))):
        t = _sigmoid(_dot(u, wg_ref[...])) * _dot(y_ref[...], wb_ref[j])
        z = t if z is None else z + t
    o_ref[...] += _dot(z.astype(BF16), wo_ref[...])

    @pl.when(n == n_tiles - 1)
    def _():
        o_ref[...] = x_ref[...] + o_ref[...]


def _merge(x, u, ya, yb, yc, wg, wb, wo, *, layer, tm=512, tn=512):
    s, d = x.shape
    n_tiles = d // tn
    br = pl.BlockSpec((tm, 1024), lambda i, n: (i, 0))
    gate_specs = [pl.BlockSpec((d, tn), functools.partial(lambda i, n, j: (0, j * n_tiles + n), j=j))
                  for j in range(N_BRANCH)]
    return pl.pallas_call(
        functools.partial(_merge_body, n_tiles=n_tiles),
        grid=(s // tm, n_tiles),
        in_specs=[pl.BlockSpec((tm, d), lambda i, n: (i, 0)),
                  pl.BlockSpec((tm, d), lambda i, n: (i, 0)),
                  br, br, br,
                  *gate_specs,
                  pl.BlockSpec((None, N_BRANCH, 1024, tn), lambda i, n: (layer, 0, 0, n)),
                  pl.BlockSpec((None, tn, d), lambda i, n: (layer, n, 0))],
        out_specs=pl.BlockSpec((tm, d), lambda i, n: (i, 0)),
        out_shape=jax.ShapeDtypeStruct((s, d), F32),
        compiler_params=_cparams("parallel", "arbitrary"),
        name="merge",
    )(x, u, ya, yb, yc, wg, wg, wg, wb, wo)


def _rot_cols(w):
    half = B_ROPE // 2
    return jnp.concatenate([-w[..., half:], w[..., :half]], axis=-1)


def _prep_layer(w_in, gate_bias, w_uq, w_ukv, lru_w_a, lru_w_x):
    d = w_in.shape[0]
    w1 = jnp.concatenate([w_in[:, _OFF_Q:_OFF_K] * (A_DQK ** -0.5), w_in[:, _OFF_K:_OFF_I]], axis=1).astype(BF16)
    w2 = jnp.concatenate([w_in[:, _OFF_O:_OFF_CQ], w_in[:, _OFF_CX:_OFF_G]], axis=1).astype(BF16)
    w_kr = w_in[:, _OFF_KR:_OFF_CX]
    w3 = jnp.concatenate([w_in[:, _OFF_CQ:_OFF_CKV], w_kr, _rot_cols(w_kr),
                          w_in[:, _OFF_CKV:_OFF_KR]], axis=1).astype(BF16)
    w_if = w_in[:, _OFF_I:_OFF_O]
    wc = jnp.concatenate([w_if, jnp.zeros((d, 120), F32)], axis=1).astype(BF16)
    wr = jnp.concatenate([w_if.T, jnp.zeros((8, d), F32)], axis=0).astype(BF16)
    bc = jnp.concatenate([gate_bias, jnp.zeros((120,), F32)])[None, :]
    br = jnp.concatenate([gate_bias, jnp.zeros((8,), F32)])[:, None]

    scale = (B_NOPE + B_ROPE) ** -0.5 * LOG2_E
    wq3 = w_uq.reshape(B_Q_LORA, B_HEADS, B_NOPE + B_ROPE)
    q_nope = wq3[:, :, :B_NOPE].reshape(B_Q_LORA, B_HEADS * B_NOPE)
    q_rope = wq3[:, :, B_NOPE:]
    q_rr = jnp.concatenate([q_rope, _rot_cols(q_rope)], axis=-1).reshape(B_Q_LORA, B_HEADS * 128)
    wqt = (jnp.concatenate([q_nope, q_rr], axis=1) * scale).T.astype(BF16)
    wkv3 = w_ukv.reshape(B_KV_LORA, B_HEADS, B_NOPE + B_DV)
    wk = wkv3[:, :, :B_NOPE].reshape(B_KV_LORA, -1).astype(BF16)
    wvt = wkv3[:, :, B_NOPE:].reshape(B_KV_LORA, -1).T.astype(BF16)
    wax = jnp.concatenate([lru_w_a, lru_w_x], axis=-1).astype(BF16)
    wg = w_in[:, _OFF_G:].astype(BF16)
    return dict(w1=w1, w2=w2, w3=w3, wc=wc, wr=wr, bc=bc, br=br, wqt=wqt, wk=wk, wvt=wvt, wax=wax, wg=wg)


def kernel(x, ffn1_norm, ffn1_w_gate, ffn1_w_up, ffn1_w_down, mix_norm, w_in, mlstm_gate_bias, mlstm_out_norm, mla_q_norm, mla_w_uq, mla_kv_norm, mla_w_ukv, lru_conv_w, lru_conv_b, lru_w_a, lru_b_a, lru_w_x, lru_b_x, lru_lambda, w_branch, w_out, ffn2_norm, ffn2_w_gate, ffn2_w_up, ffn2_w_down, final_norm):
    b, s, d = x.shape
    depth = w_in.shape[0]
    pos = jnp.arange(s, dtype=F32)
    inv_freq = jnp.power(ROPE_BASE, -jnp.arange(0, B_ROPE, 2, dtype=F32) / B_ROPE)
    ang = pos[:, None] * inv_freq[None, :]
    cos, sin = jnp.cos(ang), jnp.sin(ang)
    cs = jnp.concatenate([cos, cos, sin, sin], axis=1)
    cst = cs.T

    f1g, f1u = _cast_bf16(ffn1_w_gate, rows=256), _cast_bf16(ffn1_w_up, rows=256)
    f2g, f2u = _cast_bf16(ffn2_w_gate, rows=256), _cast_bf16(ffn2_w_up, rows=256)
    f1d, f2d = _cast_bf16(ffn1_w_down, rows=512), _cast_bf16(ffn2_w_down, rows=512)
    wb_all, wo_all = _cast_bf16(w_branch, rows=512), _cast_bf16(w_out, rows=512)
    layer_w = [_prep_layer(w_in[l], mlstm_gate_bias[l], mla_w_uq[l], mla_w_ukv[l], lru_w_a[l], lru_w_x[l])
               for l in range(depth)]

    outs = []
    for bi in range(b):
        xb = x[bi]
        for l in range(depth):
            w = layer_w[l]
            xb, u = _ffn(xb, ffn1_norm[l][None], f1g, f1u, f1d, mix_norm[l][None], layer=l, mode="mix")
            p1 = _matmul(u, w["w1"], BF16, name="proj_qkv")
            p2 = _matmul(u, w["w2"], F32, name="proj_o_cx")
            p3 = _matmul(u, w["w3"], F32, tn=768, name="proj_mla")
            gc, gr = _gates(u, w["wc"], w["wr"], w["bc"], w["br"])
            ya = _mlstm(p1, gc, gr, p2, mlstm_out_norm[l][None])
            qt, kh, vt = _mla_prep(p3, mla_q_norm[l][None], mla_kv_norm[l][None], w["wqt"], w["wk"],
                                   w["wvt"], cs, cst)
            yb = _flash(qt, kh, vt)
            yc = _lru(p2, lru_conv_w[l], lru_conv_b[l][None], w["wax"], lru_b_a[l][None],
                      lru_b_x[l][None], lru_lambda[l][None])
            xb = _merge(xb, u, ya, yb, yc, w["wg"], wb_all, wo_all, layer=l)
            last = l == depth - 1
            xb = _ffn(xb, ffn2_norm[l][None], f2g, f2u, f2d, final_norm[None] if last else ffn2_norm[l][None],
                      layer=l, mode="final" if last else "plain")
        outs.append(xb[None])
    return outs[0] if b == 1 else jnp.concatenate(outs, axis=0)
```

```python
import functools

import jax
import jax.numpy as jnp
from jax import lax
from jax.experimental import pallas as pl
from jax.experimental.pallas import tpu as pltpu

F32 = jnp.float32
BF16 = jnp.bfloat16

D_MODEL = 2048
D_FF = 5632
NORM_EPS = 1e-6
NEG = -1e30
LOG2_E = 1.4426950408889634

A_HEADS = 4
A_DQK = 128
A_DV = 256
A_WIDTH = A_HEADS * A_DV

B_HEADS = 8
B_Q_LORA = 384
B_KV_LORA = 256
B_NOPE = 128
B_ROPE = 64
B_DV = 128
B_DK_PAD = 256
ROPE_BASE = 10000.0

C_WIDTH = 1024
C_BLOCKS = 8
C_BLOCK_DIM = C_WIDTH // C_BLOCKS
C_CONV = 4
C_POW = 8.0

N_BRANCH = 3

_OFF_Q = 0
_OFF_K = 512
_OFF_V = 1024
_OFF_I = 2048
_OFF_O = 2056
_OFF_CQ = 3080
_OFF_CKV = 3464
_OFF_KR = 3720
_OFF_CX = 3784
_OFF_G = 4808

VMEM_LIMIT_BYTES = 56 * 1024 * 1024


def _cparams(*sem):
    return pltpu.CompilerParams(dimension_semantics=sem, vmem_limit_bytes=VMEM_LIMIT_BYTES)


def _sigmoid(x):
    return 1.0 / (1.0 + jnp.exp(-x))


def _softplus(x):
    return jnp.maximum(x, 0.0) + jnp.log1p(jnp.exp(-jnp.abs(x)))


def _rms(x, g):
    return x * lax.rsqrt(jnp.mean(x * x, axis=-1, keepdims=True) + NORM_EPS) * g


def _dot(a, b, **kw):
    return jnp.dot(a, b, preferred_element_type=F32, **kw)


def _dot_nt(a, b):
    return lax.dot_general(a, b, (((1,), (1,)), ((), ())), preferred_element_type=F32)


def _dot_tn(a, b):
    return lax.dot_general(a, b, (((0,), (0,)), ((), ())), preferred_element_type=F32)


def _ffn_body(x_ref, g_ref, wg_ref, wu_ref, wd_ref, g2_ref, *rest, n_ff, mode):
    if mode == "mix":
        o_ref, u_ref, xn_ref = rest
    else:
        o_ref, xn_ref = rest
    j = pl.program_id(1)

    @pl.when(j == 0)
    def _():
        xn_ref[...] = _rms(x_ref[...], g_ref[...]).astype(BF16)
        o_ref[...] = jnp.zeros_like(o_ref)

    xn = xn_ref[...]
    h1 = _dot(xn, wg_ref[...])
    h2 = _dot(xn, wu_ref[...])
    h = (h1 * _sigmoid(h1) * h2).astype(BF16)
    o_ref[...] += _dot(h, wd_ref[...])

    @pl.when(j == n_ff - 1)
    def _():
        y = x_ref[...] + 0.5 * o_ref[...]
        if mode == "final":
            o_ref[...] = _rms(y, g2_ref[...])
        else:
            o_ref[...] = y
        if mode == "mix":
            u_ref[...] = _rms(y, g2_ref[...]).astype(BF16)


def _cast_body(x_ref, o_ref):
    o_ref[...] = x_ref[...].astype(o_ref.dtype)


def _cast_bf16(w, *, rows):
    shape = w.shape
    w2 = w.reshape(-1, shape[-1])
    r, c = w2.shape
    out = pl.pallas_call(
        _cast_body,
        grid=(r // rows,),
        in_specs=[pl.BlockSpec((rows, c), lambda i: (i, 0))],
        out_specs=pl.BlockSpec((rows, c), lambda i: (i, 0)),
        out_shape=jax.ShapeDtypeStruct((r, c), BF16),
        compiler_params=_cparams("parallel"),
        name="cast_bf16",
    )(w2)
    return out.reshape(shape)


def _ffn(x, g, wg, wu, wd, g2, *, layer, mode, tm=512, tf=512):
    s, d = x.shape
    n_ff = wg.shape[2] // tf
    out_shape = [jax.ShapeDtypeStruct((s, d), F32)]
    out_specs = [pl.BlockSpec((tm, d), lambda i, j: (i, 0))]
    if mode == "mix":
        out_shape.append(jax.ShapeDtypeStruct((s, d), BF16))
        out_specs.append(pl.BlockSpec((tm, d), lambda i, j: (i, 0)))
    res = pl.pallas_call(
        functools.partial(_ffn_body, n_ff=n_ff, mode=mode),
        grid=(s // tm, n_ff),
        in_specs=[
            pl.BlockSpec((tm, d), lambda i, j: (i, 0)),
            pl.BlockSpec((1, d), lambda i, j: (0, 0)),
            pl.BlockSpec((None, d, tf), lambda i, j: (layer, 0, j)),
            pl.BlockSpec((None, d, tf), lambda i, j: (layer, 0, j)),
            pl.BlockSpec((None, tf, d), lambda i, j: (layer, j, 0)),
            pl.BlockSpec((1, d), lambda i, j: (0, 0)),
        ],
        out_specs=out_specs,
        out_shape=out_shape,
        scratch_shapes=[pltpu.VMEM((tm, d), BF16)],
        compiler_params=_cparams("parallel", "arbitrary"),
        name="ffn_" + mode,
    )(x, g, wg, wu, wd, g2)
    return res if mode == "mix" else res[0]


def _mm_body(a_ref, w_ref, o_ref):
    o_ref[...] = _dot(a_ref[...], w_ref[...]).astype(o_ref.dtype)


def _matmul(a, w, out_dtype, *, tm=1024, tn=512, name="proj"):
    s, k = a.shape
    n = w.shape[1]
    tn = min(tn, n)
    return pl.pallas_call(
        _mm_body,
        grid=(s // tm, n // tn),
        in_specs=[pl.BlockSpec((tm, k), lambda i, j: (i, 0)),
                  pl.BlockSpec((k, tn), lambda i, j: (0, j))],
        out_specs=pl.BlockSpec((tm, tn), lambda i, j: (i, j)),
        out_shape=jax.ShapeDtypeStruct((s, n), out_dtype),
        compiler_params=_cparams("parallel", "arbitrary"),
        name=name,
    )(a, w)


def _log_sigmoid(x):
    return jnp.minimum(x, 0.0) - jnp.log1p(jnp.exp(-jnp.abs(x)))


def _gates_body(u_ref, wc_ref, bc_ref, gc_ref, gr_ref):
    gc = _dot(u_ref[...], wc_ref[...]) + bc_ref[...]
    lane = lax.broadcasted_iota(jnp.int32, gc.shape, 1)
    gc = jnp.where(lane >= A_HEADS, _log_sigmoid(gc), gc)
    gc_ref[...] = gc
    gr_ref[...] = gc.T[0:gr_ref.shape[0], :]


def _gates(u, wc, bc, *, tm=1024):
    s, d = u.shape
    return pl.pallas_call(
        _gates_body,
        grid=(s // tm,),
        in_specs=[pl.BlockSpec((tm, d), lambda i: (i, 0)),
                  pl.BlockSpec((d, 128), lambda i: (0, 0)),
                  pl.BlockSpec((1, 128), lambda i: (0, 0))],
        out_specs=[pl.BlockSpec((tm, 128), lambda i: (i, 0)),
                   pl.BlockSpec((16, tm), lambda i: (0, i))],
        out_shape=[jax.ShapeDtypeStruct((s, 128), F32),
                   jax.ShapeDtypeStruct((16, s), F32)],
        compiler_params=_cparams("parallel"),
        name="mlstm_gates",
    )(u, wc, bc)


def _mlstm_body(q_ref, k_ref, v_ref, gc_ref, gr_ref, o_ref, nw_ref, y_ref,
                c_ref, n_ref, m_ref, *, chunk):
    L = chunk

    @pl.when(pl.program_id(0) == 0)
    def _():
        c_ref[...] = jnp.zeros_like(c_ref)
        n_ref[...] = jnp.zeros_like(n_ref)
        m_ref[...] = jnp.full_like(m_ref, NEG)

    row = lax.broadcasted_iota(jnp.int32, (L, L), 0)
    col = lax.broadcasted_iota(jnp.int32, (L, L), 1)
    causal = col <= row
    tril = causal.astype(F32)
    triu = (row <= col).astype(F32)
    gc = gc_ref[...]
    gr = gr_ref[...]
    cum_c = _dot(tril, gc, precision=lax.Precision.HIGHEST)
    cum_r = _dot(gr, triu, precision=lax.Precision.HIGHEST)

    for h in range(A_HEADS):
        bcol = cum_c[:, A_HEADS + h:A_HEADS + h + 1]
        brow = cum_r[A_HEADS + h:A_HEADS + h + 1, :]
        icol = gc[:, h:h + 1]
        irow = gr[h:h + 1, :]
        btot = brow[:, L - 1:L]
        m_prev = m_ref[h][0:1, 0:1]

        d_intra = jnp.where(causal, bcol - brow + irow, NEG)
        d_inter = bcol + m_prev
        m_t = jnp.maximum(jnp.max(d_intra, axis=1, keepdims=True), d_inter)
        w_intra = jnp.exp(d_intra - m_t)
        w_inter = jnp.exp(d_inter - m_t)

        qh = q_ref[:, h * A_DQK:(h + 1) * A_DQK]
        kh = k_ref[:, h * A_DQK:(h + 1) * A_DQK]
        vh = v_ref[:, h * A_DV:(h + 1) * A_DV]
        c_prev = c_ref[h]
        n_prev = n_ref[h][0:1, :]

        sc = _dot_nt(qh, kh) * w_intra
        num = _dot(sc.astype(BF16), vh) + w_inter * _dot(qh, c_prev.astype(BF16))
        qn = jnp.sum(qh.astype(F32) * n_prev, axis=1, keepdims=True)
        den = jnp.sum(sc, axis=1, keepdims=True) + w_inter * qn
        h_out = num / jnp.maximum(jnp.abs(den), jnp.exp(-m_t))

        d_state_r = btot - brow + irow
        m_new = jnp.maximum(btot + m_prev, jnp.max(d_state_r, axis=1, keepdims=True))
        w_state = jnp.exp(btot - bcol + icol - m_new)
        w_prev = jnp.exp(btot + m_prev - m_new)
        kw = kh.astype(F32) * w_state
        c_ref[h] = w_prev * c_prev + _dot_tn(kw.astype(BF16), vh)
        n_new = w_prev * n_prev + jnp.sum(kw, axis=0, keepdims=True)
        n_ref[h] = jnp.broadcast_to(n_new, n_ref.shape[1:])
        m_ref[h] = jnp.broadcast_to(m_new, m_ref.shape[1:])

        sl = slice(h * A_DV, (h + 1) * A_DV)
        hn = _rms(h_out, nw_ref[:, sl])
        y_ref[:, sl] = (_sigmoid(o_ref[:, sl]) * hn).astype(BF16)


def _mlstm(p1, gc, gr, p2, nw, *, chunk=256):
    s = p1.shape[0]
    L = chunk
    return pl.pallas_call(
        functools.partial(_mlstm_body, chunk=L),
        grid=(s // L,),
        in_specs=[pl.BlockSpec((L, 512), lambda c: (c, 0)),
                  pl.BlockSpec((L, 512), lambda c: (c, 1)),
                  pl.BlockSpec((L, 1024), lambda c: (c, 1)),
                  pl.BlockSpec((L, 128), lambda c: (c, 0)),
                  pl.BlockSpec((16, L), lambda c: (0, c)),
                  pl.BlockSpec((L, 1024), lambda c: (c, 0)),
                  pl.BlockSpec((1, A_WIDTH), lambda c: (0, 0))],
        out_specs=pl.BlockSpec((L, A_WIDTH), lambda c: (c, 0)),
        out_shape=jax.ShapeDtypeStruct((s, A_WIDTH), BF16),
        scratch_shapes=[pltpu.VMEM((A_HEADS, A_DQK, A_DV), F32),
                        pltpu.VMEM((A_HEADS, 8, A_DQK), F32),
                        pltpu.VMEM((A_HEADS, 8, 128), F32)],
        compiler_params=_cparams("arbitrary"),
        name="mlstm",
    )(p1, p1, p1, gc, gr, p2, nw)


FLASH_BLOCK = 512
FLASH_QTILE = 1024


def _mla_prep_body(p3_ref, qn_ref, kvn_ref, wqt_ref, wk_ref, wvt_ref, cs_ref, cst_ref,
                   qt_out, k_out, vt_out):
    p3 = p3_ref[...]
    cq = _rms(p3[:, 0:B_Q_LORA], qn_ref[...]).astype(BF16)
    ckv = _rms(p3[:, 512:768], kvn_ref[...]).astype(BF16)
    cs = cs_ref[...]
    cst = cst_ref[...]
    low = lax.broadcasted_iota(jnp.int32, cs.shape, 1) < B_ROPE

    tt = p3[:, 384:512] * cs
    kr = jnp.where(low, tt + pltpu.roll(tt, B_ROPE, axis=1), 0.0).astype(BF16)
    kn = _dot(ckv, wk_ref[...])
    qt = _dot_nt(wqt_ref[...], cq)
    vt = _dot_nt(wvt_ref[...], ckv)
    zeros = jnp.zeros((B_ROPE, cst.shape[1]), BF16)
    for h in range(B_HEADS):
        k_out[h, :, 0:128] = kn[:, h * 128:(h + 1) * 128].astype(BF16)
        k_out[h, :, 128:256] = kr
        qt_out[h, 0:128, :] = qt[h * 128:(h + 1) * 128, :].astype(BF16)
        tq = qt[1024 + h * 128:1024 + (h + 1) * 128, :] * cst
        qt_out[h, 128:192, :] = (tq[0:B_ROPE, :] + tq[B_ROPE:2 * B_ROPE, :]).astype(BF16)
        qt_out[h, 192:256, :] = zeros
        vt_out[h, 0] = vt[h * 128:(h + 1) * 128, :].astype(BF16)


def _mla_prep(p3, qn, kvn, wqt, wk, wvt, cs, cst):
    s = p3.shape[0]
    tm = FLASH_BLOCK
    return pl.pallas_call(
        _mla_prep_body,
        grid=(s // tm,),
        in_specs=[pl.BlockSpec((tm, 768), lambda i: (i, 0)),
                  pl.BlockSpec((1, B_Q_LORA), lambda i: (0, 0)),
                  pl.BlockSpec((1, B_KV_LORA), lambda i: (0, 0)),
                  pl.BlockSpec((2048, B_Q_LORA), lambda i: (0, 0)),
                  pl.BlockSpec((B_KV_LORA, 1024), lambda i: (0, 0)),
                  pl.BlockSpec((1024, B_KV_LORA), lambda i: (0, 0)),
                  pl.BlockSpec((tm, 128), lambda i: (i, 0)),
                  pl.BlockSpec((128, tm), lambda i: (0, i))],
        out_specs=[pl.BlockSpec((B_HEADS, B_DK_PAD, tm), lambda i: (0, 0, i)),
                   pl.BlockSpec((B_HEADS, tm, B_DK_PAD), lambda i: (0, i, 0)),
                   pl.BlockSpec((B_HEADS, 1, B_DV, tm), lambda i: (0, i, 0, 0))],
        out_shape=[jax.ShapeDtypeStruct((B_HEADS, B_DK_PAD, s), BF16),
                   jax.ShapeDtypeStruct((B_HEADS, s, B_DK_PAD), BF16),
                   jax.ShapeDtypeStruct((B_HEADS, s // tm, B_DV, tm), BF16)],
        compiler_params=_cparams("parallel"),
        name="mla_prep",
    )(p3, qn, kvn, wqt, wk, wvt, cs, cst)


def _flash_body(qt_ref, k_ref, vt_ref, o_ref, sa_ref, sb_ref, mxa_ref, mxb_ref, acc_ref, m_ref, l_ref,
                *, blk, tq):
    qi = pl.program_id(1)
    n_sub = tq // blk
    qt = qt_ref[...]
    slot_a = (sa_ref, mxa_ref)
    slot_b = (sb_ref, mxb_ref)

    def scores(kb, slot, diag=None):
        s_ref, mx_ref = slot
        off = pl.multiple_of(kb * blk, blk)
        s = _dot(k_ref[pl.ds(off, blk), :], qt)
        if diag is not None:
            key = lax.broadcasted_iota(jnp.int32, (blk, tq), 0) + diag * blk
            qry = lax.broadcasted_iota(jnp.int32, (blk, tq), 1)
            s = jnp.where(key <= qry, s, NEG)
        s_ref[...] = s
        mx_ref[...] = jnp.broadcast_to(jnp.max(s, axis=0, keepdims=True), mx_ref.shape)

    def softmax_pv(kb, slot):
        s_ref, mx_ref = slot
        m_prev = m_ref[0:1, :]
        m_new = jnp.maximum(m_prev, mx_ref[0:1, :])
        alpha = jnp.exp2(m_prev - m_new)
        p = jnp.exp2(s_ref[...] - m_new)
        l_ref[...] = jnp.broadcast_to(alpha * l_ref[0:1, :] + jnp.sum(p, axis=0, keepdims=True), l_ref.shape)
        m_ref[...] = jnp.broadcast_to(m_new, m_ref.shape)
        acc_ref[...] = alpha * acc_ref[...] + _dot(vt_ref[kb], p.astype(BF16))

    m_ref[...] = jnp.full_like(m_ref, NEG)
    l_ref[...] = jnp.zeros_like(l_ref)
    acc_ref[...] = jnp.zeros_like(acc_ref)

    def pair(j, kb_cur):
        scores(j, slot_b)
        softmax_pv(kb_cur, slot_a)
        scores(j + 1, slot_a)
        softmax_pv(j, slot_b)
        return j + 1

    def quad(jj, kb_cur):
        return pair(4 * jj + 2, pair(4 * jj, kb_cur))

    d0 = n_sub * qi
    scores(d0, slot_b, diag=0)
    scores(d0 + 1, slot_a, diag=1)
    softmax_pv(d0, slot_b)

    n_full = n_sub * qi
    n_quad = n_full // 4
    kb_cur = lax.fori_loop(0, n_quad, quad, d0 + 1)

    @pl.when(n_full - 4 * n_quad == 0)
    def _():
        softmax_pv(kb_cur, slot_a)

    @pl.when(n_full - 4 * n_quad == 2)
    def _():
        softmax_pv(pair(n_full - 2, kb_cur), slot_a)

    o = acc_ref[...] / l_ref[0:1, :]
    o_ref[...] = o.T.astype(BF16)


def _flash(qt, k, vt):
    nh, dk, s = qt.shape
    blk, tq = FLASH_BLOCK, FLASH_QTILE
    assert tq == 2 * blk
    return pl.pallas_call(
        functools.partial(_flash_body, blk=blk, tq=tq),
        grid=(nh, s // tq),
        in_specs=[pl.BlockSpec((None, dk, tq), lambda h, i: (h, 0, i)),
                  pl.BlockSpec((None, s, dk), lambda h, i: (h, 0, 0)),
                  pl.BlockSpec((None, s // blk, B_DV, blk), lambda h, i: (h, 0, 0, 0))],
        out_specs=pl.BlockSpec((tq, B_DV), lambda h, i: (i, h)),
        out_shape=jax.ShapeDtypeStruct((s, nh * B_DV), BF16),
        scratch_shapes=[pltpu.VMEM((blk, tq), F32),
                        pltpu.VMEM((blk, tq), F32),
                        pltpu.VMEM((8, tq), F32),
                        pltpu.VMEM((8, tq), F32),
                        pltpu.VMEM((B_DV, tq), F32),
                        pltpu.VMEM((8, tq), F32),
                        pltpu.VMEM((8, tq), F32)],
        compiler_params=_cparams("parallel", "arbitrary"),
        name="mla_flash",
    )(qt, k, vt)


def _lru_body(cx_ref, cw_ref, cb_ref, wax_ref, ba_ref, bx_ref, lam_ref, y_ref,
              xs_ref, h_ref, *, tb):
    T = tb

    @pl.when(pl.program_id(0) == 0)
    def _():
        xs_ref[0:8, :] = jnp.zeros((8, C_WIDTH), F32)
        h_ref[...] = jnp.zeros_like(h_ref)

    x = cx_ref[...]
    xs_ref[8:8 + T, :] = x
    w = cw_ref[...]
    xc = (w[3:4, :] * x + w[2:3, :] * xs_ref[7:7 + T, :] + w[1:2, :] * xs_ref[6:6 + T, :]
          + w[0:1, :] * xs_ref[5:5 + T, :] + cb_ref[...])
    xs_ref[0:8, :] = x[T - 8:T, :]

    ra, rx = [], []
    for b in range(C_BLOCKS):
        xb = xc[:, b * C_BLOCK_DIM:(b + 1) * C_BLOCK_DIM].astype(BF16)
        g = _dot(xb, wax_ref[b])
        ra.append(g[:, :C_BLOCK_DIM])
        rx.append(g[:, C_BLOCK_DIM:])
    r = _sigmoid(jnp.concatenate(ra, axis=1) + ba_ref[...])
    gi = _sigmoid(jnp.concatenate(rx, axis=1) + bx_ref[...])
    log_a = -C_POW * r * _softplus(-lam_ref[...])
    a = jnp.exp(log_a)
    u = jnp.sqrt(-jnp.tanh(log_a) * (a * a + 1.0)) * (gi * xc)

    G = T // 8
    a = a.reshape(G, 8, C_WIDTH)
    u = u.reshape(G, 8, C_WIDTH)
    sub = lax.broadcasted_iota(jnp.int32, (G, 8, C_WIDTH), 1)
    for d in (1, 2, 4):
        keep = sub >= d
        a_s = jnp.where(keep, pltpu.roll(a, d, axis=1), 1.0)
        u_s = jnp.where(keep, pltpu.roll(u, d, axis=1), 0.0)
        u = a * u_s + u
        a = a * a_s
    h_in = h_ref[0:1, :]
    groups = []
    for g in range(G):
        hg = a[g] * h_in + u[g]
        groups.append(hg)
        h_in = hg[7:8, :]
    h_ref[...] = jnp.broadcast_to(h_in, h_ref.shape)
    y_ref[...] = jnp.concatenate(groups, axis=0).astype(BF16)


def _lru(p2, cw, cb, wax, ba, bx, lam, *, tb=256):
    s = p2.shape[0]
    vec = pl.BlockSpec((1, C_WIDTH), lambda c: (0, 0))
    return pl.pallas_call(
        functools.partial(_lru_body, tb=tb),
        grid=(s // tb,),
        in_specs=[pl.BlockSpec((tb, C_WIDTH), lambda c: (c, 1)),
                  pl.BlockSpec((C_CONV, C_WIDTH), lambda c: (0, 0)),
                  vec,
                  pl.BlockSpec((C_BLOCKS, C_BLOCK_DIM, 2 * C_BLOCK_DIM), lambda c: (0, 0, 0)),
                  vec, vec, vec],
        out_specs=pl.BlockSpec((tb, C_WIDTH), lambda c: (c, 0)),
        out_shape=jax.ShapeDtypeStruct((s, C_WIDTH), BF16),
        scratch_shapes=[pltpu.VMEM((tb + 8, C_WIDTH), F32),
                        pltpu.VMEM((8, C_WIDTH), F32)],
        compiler_params=_cparams("arbitrary"),
        name="rglru",
    )(p2, cw, cb, wax, ba, bx, lam)


def _merge_body(x_ref, u_ref, ya_ref, yb_ref, yc_ref, wga_ref, wgb_ref, wgc_ref, wb_ref, wo_ref, o_ref,
                *, n_tiles):
    n = pl.program_id(1)

    @pl.when(n == 0)
    def _():
        o_ref[...] = jnp.zeros_like(o_ref)

    u = u_ref[...]
    z = None
    for j, (y_ref, wg_ref) in enumerate(((ya_ref, wga_ref), (yb_ref, wgb_ref), (yc_ref, wgc_ref))):
        t = _sigmoid(_dot(u, wg_ref[...])) * _dot(y_ref[...], wb_ref[j])
        z = t if z is None else z + t
    o_ref[...] += _dot(z.astype(BF16), wo_ref[...])

    @pl.when(n == n_tiles - 1)
    def _():
        o_ref[...] = x_ref[...] + o_ref[...]


def _merge(x, u, ya, yb, yc, wg, wb, wo, *, layer, tm=512, tn=512):
    s, d = x.shape
    n_tiles = d // tn
    br = pl.BlockSpec((tm, 1024), lambda i, n: (i, 0))
    gate_specs = [pl.BlockSpec((d, tn), functools.partial(lambda i, n, j: (0, j * n_tiles + n), j=j))
                  for j in range(N_BRANCH)]
    return pl.pallas_call(
        functools.partial(_merge_body, n_tiles=n_tiles),
        grid=(s // tm, n_tiles),
        in_specs=[pl.BlockSpec((tm, d), lambda i, n: (i, 0)),
                  pl.BlockSpec((tm, d), lambda i, n: (i, 0)),
                  br, br, br,
                  *gate_specs,
                  pl.BlockSpec((None, N_BRANCH, 1024, tn), lambda i, n: (layer, 0, 0, n)),
                  pl.BlockSpec((None, tn, d), lambda i, n: (layer, n, 0))],
        out_specs=pl.BlockSpec((tm, d), lambda i, n: (i, 0)),
        out_shape=jax.ShapeDtypeStruct((s, d), F32),
        compiler_params=_cparams("parallel", "arbitrary"),
        name="merge",
    )(x, u, ya, yb, yc, wg, wg, wg, wb, wo)


def _rot_cols(w):
    half = B_ROPE // 2
    return jnp.concatenate([-w[..., half:], w[..., :half]], axis=-1)


def _prep_layer(w_in, gate_bias, w_uq, w_ukv, lru_w_a, lru_w_x):
    d = w_in.shape[0]
    w1 = jnp.concatenate([w_in[:, _OFF_Q:_OFF_K] * (A_DQK ** -0.5), w_in[:, _OFF_K:_OFF_I]], axis=1).astype(BF16)
    w2 = jnp.concatenate([w_in[:, _OFF_O:_OFF_CQ], w_in[:, _OFF_CX:_OFF_G]], axis=1).astype(BF16)
    w_kr = w_in[:, _OFF_KR:_OFF_CX]
    w3 = jnp.concatenate([w_in[:, _OFF_CQ:_OFF_CKV], w_kr, _rot_cols(w_kr),
                          w_in[:, _OFF_CKV:_OFF_KR]], axis=1).astype(BF16)
    w_if = w_in[:, _OFF_I:_OFF_O]
    wc = jnp.concatenate([w_if, jnp.zeros((d, 120), F32)], axis=1).astype(BF16)
    bc = jnp.concatenate([gate_bias, jnp.zeros((120,), F32)])[None, :]

    scale = (B_NOPE + B_ROPE) ** -0.5 * LOG2_E
    wq3 = w_uq.reshape(B_Q_LORA, B_HEADS, B_NOPE + B_ROPE)
    q_nope = wq3[:, :, :B_NOPE].reshape(B_Q_LORA, B_HEADS * B_NOPE)
    q_rope = wq3[:, :, B_NOPE:]
    q_rr = jnp.concatenate([q_rope, _rot_cols(q_rope)], axis=-1).reshape(B_Q_LORA, B_HEADS * 128)
    wqt = (jnp.concatenate([q_nope, q_rr], axis=1) * scale).T.astype(BF16)
    wkv3 = w_ukv.reshape(B_KV_LORA, B_HEADS, B_NOPE + B_DV)
    wk = wkv3[:, :, :B_NOPE].reshape(B_KV_LORA, -1).astype(BF16)
    wvt = wkv3[:, :, B_NOPE:].reshape(B_KV_LORA, -1).T.astype(BF16)
    wax = jnp.concatenate([lru_w_a, lru_w_x], axis=-1).astype(BF16)
    wg = w_in[:, _OFF_G:].astype(BF16)
    return dict(w1=w1, w2=w2, w3=w3, wc=wc, bc=bc, wqt=wqt, wk=wk, wvt=wvt, wax=wax, wg=wg)


def kernel(x, ffn1_norm, ffn1_w_gate, ffn1_w_up, ffn1_w_down, mix_norm, w_in, mlstm_gate_bias, mlstm_out_norm, mla_q_norm, mla_w_uq, mla_kv_norm, mla_w_ukv, lru_conv_w, lru_conv_b, lru_w_a, lru_b_a, lru_w_x, lru_b_x, lru_lambda, w_branch, w_out, ffn2_norm, ffn2_w_gate, ffn2_w_up, ffn2_w_down, final_norm):
    b, s, d = x.shape
    depth = w_in.shape[0]
    pos = jnp.arange(s, dtype=F32)
    inv_freq = jnp.power(ROPE_BASE, -jnp.arange(0, B_ROPE, 2, dtype=F32) / B_ROPE)
    ang = pos[:, None] * inv_freq[None, :]
    cos, sin = jnp.cos(ang), jnp.sin(ang)
    cs = jnp.concatenate([cos, cos, sin, sin], axis=1)
    cst = cs.T

    f1g, f1u = _cast_bf16(ffn1_w_gate, rows=256), _cast_bf16(ffn1_w_up, rows=256)
    f2g, f2u = _cast_bf16(ffn2_w_gate, rows=256), _cast_bf16(ffn2_w_up, rows=256)
    f1d, f2d = _cast_bf16(ffn1_w_down, rows=512), _cast_bf16(ffn2_w_down, rows=512)
    wb_all, wo_all = _cast_bf16(w_branch, rows=512), _cast_bf16(w_out, rows=512)
    layer_w = [_prep_layer(w_in[l], mlstm_gate_bias[l], mla_w_uq[l], mla_w_ukv[l], lru_w_a[l], lru_w_x[l])
               for l in range(depth)]

    outs = []
    for bi in range(b):
        xb = x[bi]
        for l in range(depth):
            w = layer_w[l]
            xb, u = _ffn(xb, ffn1_norm[l][None], f1g, f1u, f1d, mix_norm[l][None], layer=l, mode="mix")
            p1 = _matmul(u, w["w1"], BF16, name="proj_qkv")
            p2 = _matmul(u, w["w2"], F32, name="proj_o_cx")
            p3 = _matmul(u, w["w3"], F32, tn=768, name="proj_mla")
            gc, gr = _gates(u, w["wc"], w["bc"])
            ya = _mlstm(p1, gc, gr, p2, mlstm_out_norm[l][None])
            qt, kh, vt = _mla_prep(p3, mla_q_norm[l][None], mla_kv_norm[l][None], w["wqt"], w["wk"],
                                   w["wvt"], cs, cst)
            yb = _flash(qt, kh, vt)
            yc = _lru(p2, lru_conv_w[l], lru_conv_b[l][None], w["wax"], lru_b_a[l][None],
                      lru_b_x[l][None], lru_lambda[l][None])
            xb = _merge(xb, u, ya, yb, yc, w["wg"], wb_all, wo_all, layer=l)
            last = l == depth - 1
            xb = _ffn(xb, ffn2_norm[l][None], f2g, f2u, f2d, final_norm[None] if last else ffn2_norm[l][None],
                      layer=l, mode="final" if last else "plain")
        outs.append(xb[None])
    return outs[0] if b == 1 else jnp.concatenate(outs, axis=0)
```

```python
import functools

import jax
import jax.numpy as jnp
from jax import lax
from jax.experimental import pallas as pl
from jax.experimental.pallas import tpu as pltpu

F32 = jnp.float32
BF16 = jnp.bfloat16

D_MODEL = 2048
D_FF = 5632
NORM_EPS = 1e-6
NEG = -1e30
LOG2_E = 1.4426950408889634

A_HEADS = 4
A_DQK = 128
A_DV = 256
A_WIDTH = A_HEADS * A_DV

B_HEADS = 8
B_Q_LORA = 384
B_KV_LORA = 256
B_NOPE = 128
B_ROPE = 64
B_DV = 128
B_DK_PAD = 256
ROPE_BASE = 10000.0

C_WIDTH = 1024
C_BLOCKS = 8
C_BLOCK_DIM = C_WIDTH // C_BLOCKS
C_CONV = 4
C_POW = 8.0

N_BRANCH = 3

_OFF_Q = 0
_OFF_K = 512
_OFF_V = 1024
_OFF_I = 2048
_OFF_O = 2056
_OFF_CQ = 3080
_OFF_CKV = 3464
_OFF_KR = 3720
_OFF_CX = 3784
_OFF_G = 4808

VMEM_LIMIT_BYTES = 56 * 1024 * 1024


def _cparams(*sem):
    return pltpu.CompilerParams(dimension_semantics=sem, vmem_limit_bytes=VMEM_LIMIT_BYTES)


def _sigmoid(x):
    return 1.0 / (1.0 + jnp.exp(-x))


def _softplus(x):
    return jnp.maximum(x, 0.0) + jnp.log1p(jnp.exp(-jnp.abs(x)))


def _rms(x, g):
    return x * lax.rsqrt(jnp.mean(x * x, axis=-1, keepdims=True) + NORM_EPS) * g


def _dot(a, b, **kw):
    return jnp.dot(a, b, preferred_element_type=F32, **kw)


def _dot_nt(a, b):
    return lax.dot_general(a, b, (((1,), (1,)), ((), ())), preferred_element_type=F32)


def _dot_tn(a, b):
    return lax.dot_general(a, b, (((0,), (0,)), ((), ())), preferred_element_type=F32)


def _ffn_body(x_ref, g_ref, wg_ref, wu_ref, wd_ref, g2_ref, *rest, n_ff, mode):
    if mode == "mix":
        o_ref, u_ref, xn_ref = rest
    else:
        o_ref, xn_ref = rest
    j = pl.program_id(1)

    def down(xn):
        h1 = _dot(xn, wg_ref[...])
        h2 = _dot(xn, wu_ref[...])
        h = (h1 * _sigmoid(h1) * h2).astype(BF16)
        return _dot(h, wd_ref[...])

    @pl.when(j == 0)
    def _():
        xn = _rms(x_ref[...], g_ref[...]).astype(BF16)
        xn_ref[...] = xn
        o_ref[...] = down(xn)

    @pl.when(jnp.logical_and(j > 0, j < n_ff - 1))
    def _():
        o_ref[...] += down(xn_ref[...])

    @pl.when(j == n_ff - 1)
    def _():
        y = x_ref[...] + 0.5 * (o_ref[...] + down(xn_ref[...]))
        if mode == "final":
            o_ref[...] = _rms(y, g2_ref[...])
        else:
            o_ref[...] = y
        if mode == "mix":
            u_ref[...] = _rms(y, g2_ref[...]).astype(BF16)


def _cast_body(x_ref, o_ref):
    o_ref[...] = x_ref[...].astype(o_ref.dtype)


def _cast_bf16(w, *, rows):
    shape = w.shape
    w2 = w.reshape(-1, shape[-1])
    r, c = w2.shape
    out = pl.pallas_call(
        _cast_body,
        grid=(r // rows,),
        in_specs=[pl.BlockSpec((rows, c), lambda i: (i, 0))],
        out_specs=pl.BlockSpec((rows, c), lambda i: (i, 0)),
        out_shape=jax.ShapeDtypeStruct((r, c), BF16),
        compiler_params=_cparams("parallel"),
        name="cast_bf16",
    )(w2)
    return out.reshape(shape)


def _ffn(x, g, wg, wu, wd, g2, *, layer, mode, tm=512, tf=512):
    s, d = x.shape
    n_ff = wg.shape[2] // tf
    out_shape = [jax.ShapeDtypeStruct((s, d), F32)]
    out_specs = [pl.BlockSpec((tm, d), lambda i, j: (i, 0))]
    if mode == "mix":
        out_shape.append(jax.ShapeDtypeStruct((s, d), BF16))
        out_specs.append(pl.BlockSpec((tm, d), lambda i, j: (i, 0)))
    res = pl.pallas_call(
        functools.partial(_ffn_body, n_ff=n_ff, mode=mode),
        grid=(s // tm, n_ff),
        in_specs=[
            pl.BlockSpec((tm, d), lambda i, j: (i, 0)),
            pl.BlockSpec((1, d), lambda i, j: (0, 0)),
            pl.BlockSpec((None, d, tf), lambda i, j: (layer, 0, j)),
            pl.BlockSpec((None, d, tf), lambda i, j: (layer, 0, j)),
            pl.BlockSpec((None, tf, d), lambda i, j: (layer, j, 0)),
            pl.BlockSpec((1, d), lambda i, j: (0, 0)),
        ],
        out_specs=out_specs,
        out_shape=out_shape,
        scratch_shapes=[pltpu.VMEM((tm, d), BF16)],
        compiler_params=_cparams("parallel", "arbitrary"),
        name="ffn_" + mode,
    )(x, g, wg, wu, wd, g2)
    return res if mode == "mix" else res[0]


def _mm_body(a_ref, w_ref, o_ref):
    o_ref[...] = _dot(a_ref[...], w_ref[...]).astype(o_ref.dtype)


def _matmul(a, w, out_dtype, *, tm=1024, tn=512, name="proj"):
    s, k = a.shape
    n = w.shape[1]
    tn = min(tn, n)
    return pl.pallas_call(
        _mm_body,
        grid=(s // tm, n // tn),
        in_specs=[pl.BlockSpec((tm, k), lambda i, j: (i, 0)),
                  pl.BlockSpec((k, tn), lambda i, j: (0, j))],
        out_specs=pl.BlockSpec((tm, tn), lambda i, j: (i, j)),
        out_shape=jax.ShapeDtypeStruct((s, n), out_dtype),
        compiler_params=_cparams("parallel", "arbitrary"),
        name=name,
    )(a, w)


def _log_sigmoid(x):
    return jnp.minimum(x, 0.0) - jnp.log1p(jnp.exp(-jnp.abs(x)))


def _gates_body(u_ref, wc_ref, bc_ref, gc_ref, gr_ref):
    gc = _dot(u_ref[...], wc_ref[...]) + bc_ref[...]
    lane = lax.broadcasted_iota(jnp.int32, gc.shape, 1)
    gc = jnp.where(lane >= A_HEADS, _log_sigmoid(gc), gc)
    gc_ref[...] = gc
    gr_ref[...] = gc.T[0:gr_ref.shape[0], :]


def _gates(u, wc, bc, *, tm=1024):
    s, d = u.shape
    return pl.pallas_call(
        _gates_body,
        grid=(s // tm,),
        in_specs=[pl.BlockSpec((tm, d), lambda i: (i, 0)),
                  pl.BlockSpec((d, 128), lambda i: (0, 0)),
                  pl.BlockSpec((1, 128), lambda i: (0, 0))],
        out_specs=[pl.BlockSpec((tm, 128), lambda i: (i, 0)),
                   pl.BlockSpec((16, tm), lambda i: (0, i))],
        out_shape=[jax.ShapeDtypeStruct((s, 128), F32),
                   jax.ShapeDtypeStruct((16, s), F32)],
        compiler_params=_cparams("parallel"),
        name="mlstm_gates",
    )(u, wc, bc)


def _mlstm_body(q_ref, k_ref, v_ref, gc_ref, gr_ref, o_ref, nw_ref, y_ref,
                c_ref, n_ref, m_ref, *, chunk):
    L = chunk

    @pl.when(pl.program_id(0) == 0)
    def _():
        c_ref[...] = jnp.zeros_like(c_ref)
        n_ref[...] = jnp.zeros_like(n_ref)
        m_ref[...] = jnp.full_like(m_ref, NEG)

    row = lax.broadcasted_iota(jnp.int32, (L, L), 0)
    col = lax.broadcasted_iota(jnp.int32, (L, L), 1)
    causal = col <= row
    tril = causal.astype(F32)
    triu = (row <= col).astype(F32)
    gc = gc_ref[...]
    gr = gr_ref[...]
    cum_c = _dot(tril, gc, precision=lax.Precision.HIGHEST)
    cum_r = _dot(gr, triu, precision=lax.Precision.HIGHEST)

    for h in range(A_HEADS):
        bcol = cum_c[:, A_HEADS + h:A_HEADS + h + 1]
        brow = cum_r[A_HEADS + h:A_HEADS + h + 1, :]
        icol = gc[:, h:h + 1]
        irow = gr[h:h + 1, :]
        btot = brow[:, L - 1:L]
        m_prev = m_ref[h][0:1, 0:1]

        d_intra = jnp.where(causal, bcol - brow + irow, NEG)
        d_inter = bcol + m_prev
        m_t = jnp.maximum(jnp.max(d_intra, axis=1, keepdims=True), d_inter)
        w_intra = jnp.exp(d_intra - m_t)
        w_inter = jnp.exp(d_inter - m_t)

        qh = q_ref[:, h * A_DQK:(h + 1) * A_DQK]
        kh = k_ref[:, h * A_DQK:(h + 1) * A_DQK]
        vh = v_ref[:, h * A_DV:(h + 1) * A_DV]
        c_prev = c_ref[h]
        n_prev = n_ref[h][0:1, :]

        sc = _dot_nt(qh, kh) * w_intra
        num = _dot(sc.astype(BF16), vh) + w_inter * _dot(qh, c_prev.astype(BF16))
        qn = jnp.sum(qh.astype(F32) * n_prev, axis=1, keepdims=True)
        den = jnp.sum(sc, axis=1, keepdims=True) + w_inter * qn
        h_out = num / jnp.maximum(jnp.abs(den), jnp.exp(-m_t))

        d_state_r = btot - brow + irow
        m_new = jnp.maximum(btot + m_prev, jnp.max(d_state_r, axis=1, keepdims=True))
        w_state = jnp.exp(btot - bcol + icol - m_new)
        w_prev = jnp.exp(btot + m_prev - m_new)
        kw = kh.astype(F32) * w_state
        c_ref[h] = w_prev * c_prev + _dot_tn(kw.astype(BF16), vh)
        n_new = w_prev * n_prev + jnp.sum(kw, axis=0, keepdims=True)
        n_ref[h] = jnp.broadcast_to(n_new, n_ref.shape[1:])
        m_ref[h] = jnp.broadcast_to(m_new, m_ref.shape[1:])

        sl = slice(h * A_DV, (h + 1) * A_DV)
        hn = _rms(h_out, nw_ref[:, sl])
        y_ref[:, sl] = (_sigmoid(o_ref[:, sl]) * hn).astype(BF16)


def _mlstm(p1, gc, gr, p2, nw, *, chunk=256):
    s = p1.shape[0]
    L = chunk
    return pl.pallas_call(
        functools.partial(_mlstm_body, chunk=L),
        grid=(s // L,),
        in_specs=[pl.BlockSpec((L, 512), lambda c: (c, 0)),
                  pl.BlockSpec((L, 512), lambda c: (c, 1)),
                  pl.BlockSpec((L, 1024), lambda c: (c, 1)),
                  pl.BlockSpec((L, 128), lambda c: (c, 0)),
                  pl.BlockSpec((16, L), lambda c: (0, c)),
                  pl.BlockSpec((L, 1024), lambda c: (c, 0)),
                  pl.BlockSpec((1, A_WIDTH), lambda c: (0, 0))],
        out_specs=pl.BlockSpec((L, A_WIDTH), lambda c: (c, 0)),
        out_shape=jax.ShapeDtypeStruct((s, A_WIDTH), BF16),
        scratch_shapes=[pltpu.VMEM((A_HEADS, A_DQK, A_DV), F32),
                        pltpu.VMEM((A_HEADS, 8, A_DQK), F32),
                        pltpu.VMEM((A_HEADS, 8, 128), F32)],
        compiler_params=_cparams("arbitrary"),
        name="mlstm",
    )(p1, p1, p1, gc, gr, p2, nw)


FLASH_BLOCK = 512
FLASH_QTILE = 1024


def _mla_prep_body(p3_ref, qn_ref, kvn_ref, wqt_ref, wk_ref, wvt_ref, cs_ref, cst_ref,
                   qt_out, k_out, vt_out):
    p3 = p3_ref[...]
    cq = _rms(p3[:, 0:B_Q_LORA], qn_ref[...]).astype(BF16)
    ckv = _rms(p3[:, 512:768], kvn_ref[...]).astype(BF16)
    cs = cs_ref[...]
    cst = cst_ref[...]
    low = lax.broadcasted_iota(jnp.int32, cs.shape, 1) < B_ROPE

    tt = p3[:, 384:512] * cs
    kr = jnp.where(low, tt + pltpu.roll(tt, B_ROPE, axis=1), 0.0).astype(BF16)
    kn = _dot(ckv, wk_ref[...])
    qt = _dot_nt(wqt_ref[...], cq)
    vt = _dot_nt(wvt_ref[...], ckv)
    zeros = jnp.zeros((B_ROPE, cst.shape[1]), BF16)
    for h in range(B_HEADS):
        k_out[h, :, 0:128] = kn[:, h * 128:(h + 1) * 128].astype(BF16)
        k_out[h, :, 128:256] = kr
        qt_out[h, 0:128, :] = qt[h * 128:(h + 1) * 128, :].astype(BF16)
        tq = qt[1024 + h * 128:1024 + (h + 1) * 128, :] * cst
        qt_out[h, 128:192, :] = (tq[0:B_ROPE, :] + tq[B_ROPE:2 * B_ROPE, :]).astype(BF16)
        qt_out[h, 192:256, :] = zeros
        vt_out[h, 0] = vt[h * 128:(h + 1) * 128, :].astype(BF16)


def _mla_prep(p3, qn, kvn, wqt, wk, wvt, cs, cst):
    s = p3.shape[0]
    tm = FLASH_BLOCK
    return pl.pallas_call(
        _mla_prep_body,
        grid=(s // tm,),
        in_specs=[pl.BlockSpec((tm, 768), lambda i: (i, 0)),
                  pl.BlockSpec((1, B_Q_LORA), lambda i: (0, 0)),
                  pl.BlockSpec((1, B_KV_LORA), lambda i: (0, 0)),
                  pl.BlockSpec((2048, B_Q_LORA), lambda i: (0, 0)),
                  pl.BlockSpec((B_KV_LORA, 1024), lambda i: (0, 0)),
                  pl.BlockSpec((1024, B_KV_LORA), lambda i: (0, 0)),
                  pl.BlockSpec((tm, 128), lambda i: (i, 0)),
                  pl.BlockSpec((128, tm), lambda i: (0, i))],
        out_specs=[pl.BlockSpec((B_HEADS, B_DK_PAD, tm), lambda i: (0, 0, i)),
                   pl.BlockSpec((B_HEADS, tm, B_DK_PAD), lambda i: (0, i, 0)),
                   pl.BlockSpec((B_HEADS, 1, B_DV, tm), lambda i: (0, i, 0, 0))],
        out_shape=[jax.ShapeDtypeStruct((B_HEADS, B_DK_PAD, s), BF16),
                   jax.ShapeDtypeStruct((B_HEADS, s, B_DK_PAD), BF16),
                   jax.ShapeDtypeStruct((B_HEADS, s // tm, B_DV, tm), BF16)],
        compiler_params=_cparams("parallel"),
        name="mla_prep",
    )(p3, qn, kvn, wqt, wk, wvt, cs, cst)


def _flash_body(qt_ref, k_ref, vt_ref, o_ref, sa_ref, sb_ref, mxa_ref, mxb_ref, acc_ref, m_ref, l_ref,
                *, blk, tq):
    qi = pl.program_id(1)
    n_sub = tq // blk
    qt = qt_ref[...]
    slot_a = (sa_ref, mxa_ref)
    slot_b = (sb_ref, mxb_ref)

    def scores(kb, slot, diag=None):
        s_ref, mx_ref = slot
        off = pl.multiple_of(kb * blk, blk)
        s = _dot(k_ref[pl.ds(off, blk), :], qt)
        if diag is not None:
            key = lax.broadcasted_iota(jnp.int32, (blk, tq), 0) + diag * blk
            qry = lax.broadcasted_iota(jnp.int32, (blk, tq), 1)
            s = jnp.where(key <= qry, s, NEG)
        s_ref[...] = s
        mx_ref[...] = jnp.broadcast_to(jnp.max(s, axis=0, keepdims=True), mx_ref.shape)

    def softmax_pv(kb, slot):
        s_ref, mx_ref = slot
        m_prev = m_ref[0:1, :]
        m_new = jnp.maximum(m_prev, mx_ref[0:1, :])
        alpha = jnp.exp2(m_prev - m_new)
        p = jnp.exp2(s_ref[...] - m_new)
        l_ref[...] = jnp.broadcast_to(alpha * l_ref[0:1, :] + jnp.sum(p, axis=0, keepdims=True), l_ref.shape)
        m_ref[...] = jnp.broadcast_to(m_new, m_ref.shape)
        acc_ref[...] = alpha * acc_ref[...] + _dot(vt_ref[kb], p.astype(BF16))

    m_ref[...] = jnp.full_like(m_ref, NEG)
    l_ref[...] = jnp.zeros_like(l_ref)
    acc_ref[...] = jnp.zeros_like(acc_ref)

    def pair(j, kb_cur):
        scores(j, slot_b)
        softmax_pv(kb_cur, slot_a)
        scores(j + 1, slot_a)
        softmax_pv(j, slot_b)
        return j + 1

    def quad(jj, kb_cur):
        return pair(4 * jj + 2, pair(4 * jj, kb_cur))

    d0 = n_sub * qi
    scores(d0, slot_b, diag=0)
    scores(d0 + 1, slot_a, diag=1)
    softmax_pv(d0, slot_b)

    n_full = n_sub * qi
    n_quad = n_full // 4
    kb_cur = lax.fori_loop(0, n_quad, quad, d0 + 1)

    @pl.when(n_full - 4 * n_quad == 0)
    def _():
        softmax_pv(kb_cur, slot_a)

    @pl.when(n_full - 4 * n_quad == 2)
    def _():
        softmax_pv(pair(n_full - 2, kb_cur), slot_a)

    o = acc_ref[...] / l_ref[0:1, :]
    o_ref[...] = o.T.astype(BF16)


def _flash(qt, k, vt):
    nh, dk, s = qt.shape
    blk, tq = FLASH_BLOCK, FLASH_QTILE
    assert tq == 2 * blk
    return pl.pallas_call(
        functools.partial(_flash_body, blk=blk, tq=tq),
        grid=(nh, s // tq),
        in_specs=[pl.BlockSpec((None, dk, tq), lambda h, i: (h, 0, i)),
                  pl.BlockSpec((None, s, dk), lambda h, i: (h, 0, 0)),
                  pl.BlockSpec((None, s // blk, B_DV, blk), lambda h, i: (h, 0, 0, 0))],
        out_specs=pl.BlockSpec((tq, B_DV), lambda h, i: (i, h)),
        out_shape=jax.ShapeDtypeStruct((s, nh * B_DV), BF16),
        scratch_shapes=[pltpu.VMEM((blk, tq), F32),
                        pltpu.VMEM((blk, tq), F32),
                        pltpu.VMEM((8, tq), F32),
                        pltpu.VMEM((8, tq), F32),
                        pltpu.VMEM((B_DV, tq), F32),
                        pltpu.VMEM((8, tq), F32),
                        pltpu.VMEM((8, tq), F32)],
        compiler_params=_cparams("parallel", "arbitrary"),
        name="mla_flash",
    )(qt, k, vt)


def _lru_body(cx_ref, cw_ref, cb_ref, wax_ref, ba_ref, bx_ref, lam_ref, y_ref,
              xs_ref, h_ref, *, tb):
    T = tb

    @pl.when(pl.program_id(0) == 0)
    def _():
        xs_ref[0:8, :] = jnp.zeros((8, C_WIDTH), F32)
        h_ref[...] = jnp.zeros_like(h_ref)

    x = cx_ref[...]
    xs_ref[8:8 + T, :] = x
    w = cw_ref[...]
    xc = (w[3:4, :] * x + w[2:3, :] * xs_ref[7:7 + T, :] + w[1:2, :] * xs_ref[6:6 + T, :]
          + w[0:1, :] * xs_ref[5:5 + T, :] + cb_ref[...])
    xs_ref[0:8, :] = x[T - 8:T, :]

    ra, rx = [], []
    for b in range(C_BLOCKS):
        xb = xc[:, b * C_BLOCK_DIM:(b + 1) * C_BLOCK_DIM].astype(BF16)
        g = _dot(xb, wax_ref[b])
        ra.append(g[:, :C_BLOCK_DIM])
        rx.append(g[:, C_BLOCK_DIM:])
    r = _sigmoid(jnp.concatenate(ra, axis=1) + ba_ref[...])
    gi = _sigmoid(jnp.concatenate(rx, axis=1) + bx_ref[...])
    log_a = -C_POW * r * _softplus(-lam_ref[...])
    a = jnp.exp(log_a)
    u = jnp.sqrt(-jnp.tanh(log_a) * (a * a + 1.0)) * (gi * xc)

    G = T // 8
    a = a.reshape(G, 8, C_WIDTH)
    u = u.reshape(G, 8, C_WIDTH)
    sub = lax.broadcasted_iota(jnp.int32, (G, 8, C_WIDTH), 1)
    for d in (1, 2, 4):
        keep = sub >= d
        a_s = jnp.where(keep, pltpu.roll(a, d, axis=1), 1.0)
        u_s = jnp.where(keep, pltpu.roll(u, d, axis=1), 0.0)
        u = a * u_s + u
        a = a * a_s
    h_in = h_ref[0:1, :]
    groups = []
    for g in range(G):
        hg = a[g] * h_in + u[g]
        groups.append(hg)
        h_in = hg[7:8, :]
    h_ref[...] = jnp.broadcast_to(h_in, h_ref.shape)
    y_ref[...] = jnp.concatenate(groups, axis=0).astype(BF16)


def _lru(p2, cw, cb, wax, ba, bx, lam, *, tb=256):
    s = p2.shape[0]
    vec = pl.BlockSpec((1, C_WIDTH), lambda c: (0, 0))
    return pl.pallas_call(
        functools.partial(_lru_body, tb=tb),
        grid=(s // tb,),
        in_specs=[pl.BlockSpec((tb, C_WIDTH), lambda c: (c, 1)),
                  pl.BlockSpec((C_CONV, C_WIDTH), lambda c: (0, 0)),
                  vec,
                  pl.BlockSpec((C_BLOCKS, C_BLOCK_DIM, 2 * C_BLOCK_DIM), lambda c: (0, 0, 0)),
                  vec, vec, vec],
        out_specs=pl.BlockSpec((tb, C_WIDTH), lambda c: (c, 0)),
        out_shape=jax.ShapeDtypeStruct((s, C_WIDTH), BF16),
        scratch_shapes=[pltpu.VMEM((tb + 8, C_WIDTH), F32),
                        pltpu.VMEM((8, C_WIDTH), F32)],
        compiler_params=_cparams("arbitrary"),
        name="rglru",
    )(p2, cw, cb, wax, ba, bx, lam)


def _merge_body(x_ref, u_ref, ya_ref, yb_ref, yc_ref, wga_ref, wgb_ref, wgc_ref, wb_ref, wo_ref, o_ref,
                *, n_tiles):
    n = pl.program_id(1)

    def out_tile():
        u = u_ref[...]
        z = None
        for j, (y_ref, wg_ref) in enumerate(((ya_ref, wga_ref), (yb_ref, wgb_ref), (yc_ref, wgc_ref))):
            t = _sigmoid(_dot(u, wg_ref[...])) * _dot(y_ref[...], wb_ref[j])
            z = t if z is None else z + t
        return _dot(z.astype(BF16), wo_ref[...])

    @pl.when(n == 0)
    def _():
        o_ref[...] = out_tile()

    @pl.when(jnp.logical_and(n > 0, n < n_tiles - 1))
    def _():
        o_ref[...] += out_tile()

    @pl.when(n == n_tiles - 1)
    def _():
        o_ref[...] = x_ref[...] + (o_ref[...] + out_tile())


def _merge(x, u, ya, yb, yc, wg, wb, wo, *, layer, tm=512, tn=512):
    s, d = x.shape
    n_tiles = d // tn
    br = pl.BlockSpec((tm, 1024), lambda i, n: (i, 0))
    gate_specs = [pl.BlockSpec((d, tn), functools.partial(lambda i, n, j: (0, j * n_tiles + n), j=j))
                  for j in range(N_BRANCH)]
    return pl.pallas_call(
        functools.partial(_merge_body, n_tiles=n_tiles),
        grid=(s // tm, n_tiles),
        in_specs=[pl.BlockSpec((tm, d), lambda i, n: (i, 0)),
                  pl.BlockSpec((tm, d), lambda i, n: (i, 0)),
                  br, br, br,
                  *gate_specs,
                  pl.BlockSpec((None, N_BRANCH, 1024, tn), lambda i, n: (layer, 0, 0, n)),
                  pl.BlockSpec((None, tn, d), lambda i, n: (layer, n, 0))],
        out_specs=pl.BlockSpec((tm, d), lambda i, n: (i, 0)),
        out_shape=jax.ShapeDtypeStruct((s, d), F32),
        compiler_params=_cparams("parallel", "arbitrary"),
        name="merge",
    )(x, u, ya, yb, yc, wg, wg, wg, wb, wo)


def _rot_cols(w):
    half = B_ROPE // 2
    return jnp.concatenate([-w[..., half:], w[..., :half]], axis=-1)


def _prep_layer(w_in, gate_bias, w_uq, w_ukv, lru_w_a, lru_w_x):
    d = w_in.shape[0]
    w1 = jnp.concatenate([w_in[:, _OFF_Q:_OFF_K] * (A_DQK ** -0.5), w_in[:, _OFF_K:_OFF_I]], axis=1).astype(BF16)
    w2 = jnp.concatenate([w_in[:, _OFF_O:_OFF_CQ], w_in[:, _OFF_CX:_OFF_G]], axis=1).astype(BF16)
    w_kr = w_in[:, _OFF_KR:_OFF_CX]
    w3 = jnp.concatenate([w_in[:, _OFF_CQ:_OFF_CKV], w_kr, _rot_cols(w_kr),
                          w_in[:, _OFF_CKV:_OFF_KR]], axis=1).astype(BF16)
    w_if = w_in[:, _OFF_I:_OFF_O]
    wc = jnp.concatenate([w_if, jnp.zeros((d, 120), F32)], axis=1).astype(BF16)
    bc = jnp.concatenate([gate_bias, jnp.zeros((120,), F32)])[None, :]

    scale = (B_NOPE + B_ROPE) ** -0.5 * LOG2_E
    wq3 = w_uq.reshape(B_Q_LORA, B_HEADS, B_NOPE + B_ROPE)
    q_nope = wq3[:, :, :B_NOPE].reshape(B_Q_LORA, B_HEADS * B_NOPE)
    q_rope = wq3[:, :, B_NOPE:]
    q_rr = jnp.concatenate([q_rope, _rot_cols(q_rope)], axis=-1).reshape(B_Q_LORA, B_HEADS * 128)
    wqt = (jnp.concatenate([q_nope, q_rr], axis=1) * scale).T.astype(BF16)
    wkv3 = w_ukv.reshape(B_KV_LORA, B_HEADS, B_NOPE + B_DV)
    wk = wkv3[:, :, :B_NOPE].reshape(B_KV_LORA, -1).astype(BF16)
    wvt = wkv3[:, :, B_NOPE:].reshape(B_KV_LORA, -1).T.astype(BF16)
    wax = jnp.concatenate([lru_w_a, lru_w_x], axis=-1).astype(BF16)
    wg = w_in[:, _OFF_G:].astype(BF16)
    return dict(w1=w1, w2=w2, w3=w3, wc=wc, bc=bc, wqt=wqt, wk=wk, wvt=wvt, wax=wax, wg=wg)


def kernel(x, ffn1_norm, ffn1_w_gate, ffn1_w_up, ffn1_w_down, mix_norm, w_in, mlstm_gate_bias, mlstm_out_norm, mla_q_norm, mla_w_uq, mla_kv_norm, mla_w_ukv, lru_conv_w, lru_conv_b, lru_w_a, lru_b_a, lru_w_x, lru_b_x, lru_lambda, w_branch, w_out, ffn2_norm, ffn2_w_gate, ffn2_w_up, ffn2_w_down, final_norm):
    b, s, d = x.shape
    depth = w_in.shape[0]
    pos = jnp.arange(s, dtype=F32)
    inv_freq = jnp.power(ROPE_BASE, -jnp.arange(0, B_ROPE, 2, dtype=F32) / B_ROPE)
    ang = pos[:, None] * inv_freq[None, :]
    cos, sin = jnp.cos(ang), jnp.sin(ang)
    cs = jnp.concatenate([cos, cos, sin, sin], axis=1)
    cst = cs.T

    f1g, f1u = _cast_bf16(ffn1_w_gate, rows=256), _cast_bf16(ffn1_w_up, rows=256)
    f2g, f2u = _cast_bf16(ffn2_w_gate, rows=256), _cast_bf16(ffn2_w_up, rows=256)
    f1d, f2d = _cast_bf16(ffn1_w_down, rows=512), _cast_bf16(ffn2_w_down, rows=512)
    wb_all, wo_all = _cast_bf16(w_branch, rows=512), _cast_bf16(w_out, rows=512)
    layer_w = [_prep_layer(w_in[l], mlstm_gate_bias[l], mla_w_uq[l], mla_w_ukv[l], lru_w_a[l], lru_w_x[l])
               for l in range(depth)]

    outs = []
    for bi in range(b):
        xb = x[bi]
        for l in range(depth):
            w = layer_w[l]
            xb, u = _ffn(xb, ffn1_norm[l][None], f1g, f1u, f1d, mix_norm[l][None], layer=l, mode="mix")
            p1 = _matmul(u, w["w1"], BF16, name="proj_qkv")
            p2 = _matmul(u, w["w2"], F32, name="proj_o_cx")
            p3 = _matmul(u, w["w3"], F32, tn=768, name="proj_mla")
            gc, gr = _gates(u, w["wc"], w["bc"])
            ya = _mlstm(p1, gc, gr, p2, mlstm_out_norm[l][None])
            qt, kh, vt = _mla_prep(p3, mla_q_norm[l][None], mla_kv_norm[l][None], w["wqt"], w["wk"],
                                   w["wvt"], cs, cst)
            yb = _flash(qt, kh, vt)
            yc = _lru(p2, lru_conv_w[l], lru_conv_b[l][None], w["wax"], lru_b_a[l][None],
                      lru_b_x[l][None], lru_lambda[l][None])
            xb = _merge(xb, u, ya, yb, yc, w["wg"], wb_all, wo_all, layer=l)
            last = l == depth - 1
            xb = _ffn(xb, ffn2_norm[l][None], f2g, f2u, f2d, final_norm[None] if last else ffn2_norm[l][None],
                      layer=l, mode="final" if last else "plain")
        outs.append(xb[None])
    return outs[0] if b == 1 else jnp.concatenate(outs, axis=0)
```

```python
import functools

import jax
import jax.numpy as jnp
from jax import lax
from jax.experimental import pallas as pl
from jax.experimental.pallas import tpu as pltpu

F32 = jnp.float32
BF16 = jnp.bfloat16

D_MODEL = 2048
D_FF = 5632
NORM_EPS = 1e-6
NEG = -1e30
LOG2_E = 1.4426950408889634

A_HEADS = 4
A_DQK = 128
A_DV = 256
A_WIDTH = A_HEADS * A_DV

B_HEADS = 8
B_Q_LORA = 384
B_KV_LORA = 256
B_NOPE = 128
B_ROPE = 64
B_DV = 128
B_DK_PAD = 256
ROPE_BASE = 10000.0

C_WIDTH = 1024
C_BLOCKS = 8
C_BLOCK_DIM = C_WIDTH // C_BLOCKS
C_CONV = 4
C_POW = 8.0

N_BRANCH = 3

_OFF_Q = 0
_OFF_K = 512
_OFF_V = 1024
_OFF_I = 2048
_OFF_O = 2056
_OFF_CQ = 3080
_OFF_CKV = 3464
_OFF_KR = 3720
_OFF_CX = 3784
_OFF_G = 4808

VMEM_LIMIT_BYTES = 56 * 1024 * 1024


def _cparams(*sem):
    return pltpu.CompilerParams(dimension_semantics=sem, vmem_limit_bytes=VMEM_LIMIT_BYTES)


def _sigmoid(x):
    return 1.0 / (1.0 + jnp.exp(-x))


def _softplus(x):
    return jnp.maximum(x, 0.0) + jnp.log1p(jnp.exp(-jnp.abs(x)))


def _rms(x, g):
    return x * lax.rsqrt(jnp.mean(x * x, axis=-1, keepdims=True) + NORM_EPS) * g


def _dot(a, b, **kw):
    return jnp.dot(a, b, preferred_element_type=F32, **kw)


def _dot_nt(a, b):
    return lax.dot_general(a, b, (((1,), (1,)), ((), ())), preferred_element_type=F32)


def _dot_tn(a, b):
    return lax.dot_general(a, b, (((0,), (0,)), ((), ())), preferred_element_type=F32)


def _ffn_body(x_ref, g_ref, wg_ref, wu_ref, wd_ref, g2_ref, *rest, n_ff, mode):
    if mode == "mix":
        o_ref, u_ref, xn_ref = rest
    else:
        o_ref, xn_ref = rest
    j = pl.program_id(1)

    def down(xn):
        h1 = _dot(xn, wg_ref[...])
        h2 = _dot(xn, wu_ref[...])
        h = (h1 * _sigmoid(h1) * h2).astype(BF16)
        return _dot(h, wd_ref[...])

    @pl.when(j == 0)
    def _():
        xn = _rms(x_ref[...], g_ref[...]).astype(BF16)
        xn_ref[...] = xn
        o_ref[...] = down(xn)

    @pl.when(jnp.logical_and(j > 0, j < n_ff - 1))
    def _():
        o_ref[...] += down(xn_ref[...])

    @pl.when(j == n_ff - 1)
    def _():
        y = x_ref[...] + 0.5 * (o_ref[...] + down(xn_ref[...]))
        if mode == "final":
            o_ref[...] = _rms(y, g2_ref[...])
        else:
            o_ref[...] = y
        if mode == "mix":
            u_ref[...] = _rms(y, g2_ref[...]).astype(BF16)


def _cast_body(x_ref, o_ref):
    o_ref[...] = x_ref[...].astype(o_ref.dtype)


def _cast_bf16(w, *, rows):
    shape = w.shape
    w2 = w.reshape(-1, shape[-1])
    r, c = w2.shape
    out = pl.pallas_call(
        _cast_body,
        grid=(r // rows,),
        in_specs=[pl.BlockSpec((rows, c), lambda i: (i, 0))],
        out_specs=pl.BlockSpec((rows, c), lambda i: (i, 0)),
        out_shape=jax.ShapeDtypeStruct((r, c), BF16),
        compiler_params=_cparams("parallel"),
        name="cast_bf16",
    )(w2)
    return out.reshape(shape)


def _cast_tiles_body(x_ref, o_ref):
    tf = o_ref.shape[2]
    for t in range(o_ref.shape[0]):
        o_ref[t] = x_ref[:, t * tf:(t + 1) * tf].astype(o_ref.dtype)


def _cast_bf16_col_tiles(w, *, tf, rows):
    n, r, c = w.shape
    nt = c // tf
    return pl.pallas_call(
        _cast_tiles_body,
        grid=(n, r // rows),
        in_specs=[pl.BlockSpec((None, rows, c), lambda l, i: (l, i, 0))],
        out_specs=pl.BlockSpec((None, nt, rows, tf), lambda l, i: (l, 0, i, 0)),
        out_shape=jax.ShapeDtypeStruct((n, nt, r, tf), BF16),
        compiler_params=_cparams("parallel", "parallel"),
        name="cast_bf16_tiles",
    )(w)


FFN_TILE = 512
MERGE_TILE = 512


def _ffn(x, g, wg, wu, wd, g2, *, layer, mode, tm=512):
    s, d = x.shape
    n_ff, tf = wg.shape[1], wg.shape[3]
    out_shape = [jax.ShapeDtypeStruct((s, d), F32)]
    out_specs = [pl.BlockSpec((tm, d), lambda i, j: (i, 0))]
    if mode == "mix":
        out_shape.append(jax.ShapeDtypeStruct((s, d), BF16))
        out_specs.append(pl.BlockSpec((tm, d), lambda i, j: (i, 0)))
    res = pl.pallas_call(
        functools.partial(_ffn_body, n_ff=n_ff, mode=mode),
        grid=(s // tm, n_ff),
        in_specs=[
            pl.BlockSpec((tm, d), lambda i, j: (i, 0)),
            pl.BlockSpec((1, d), lambda i, j: (0, 0)),
            pl.BlockSpec((None, None, d, tf), lambda i, j: (layer, j, 0, 0)),
            pl.BlockSpec((None, None, d, tf), lambda i, j: (layer, j, 0, 0)),
            pl.BlockSpec((None, tf, d), lambda i, j: (layer, j, 0)),
            pl.BlockSpec((1, d), lambda i, j: (0, 0)),
        ],
        out_specs=out_specs,
        out_shape=out_shape,
        scratch_shapes=[pltpu.VMEM((tm, d), BF16)],
        compiler_params=_cparams("parallel", "arbitrary"),
        name="ffn_" + mode,
    )(x, g, wg, wu, wd, g2)
    return res if mode == "mix" else res[0]


def _mm_body(a_ref, w_ref, o_ref):
    o_ref[...] = _dot(a_ref[...], w_ref[...]).astype(o_ref.dtype)


def _matmul(a, w, out_dtype, *, tm=1024, tn=512, name="proj"):
    s, k = a.shape
    n = w.shape[1]
    tn = min(tn, n)
    return pl.pallas_call(
        _mm_body,
        grid=(s // tm, n // tn),
        in_specs=[pl.BlockSpec((tm, k), lambda i, j: (i, 0)),
                  pl.BlockSpec((k, tn), lambda i, j: (0, j))],
        out_specs=pl.BlockSpec((tm, tn), lambda i, j: (i, j)),
        out_shape=jax.ShapeDtypeStruct((s, n), out_dtype),
        compiler_params=_cparams("parallel", "arbitrary"),
        name=name,
    )(a, w)


def _mm_t_body(w_ref, a_ref, o_ref):
    o_ref[...] = _dot_nt(w_ref[...], a_ref[...]).astype(o_ref.dtype)


def _matmul_t(wt, a, out_dtype, *, tm=1024, tn=512, name="proj_t"):
    n, k = wt.shape
    s = a.shape[0]
    return pl.pallas_call(
        _mm_t_body,
        grid=(s // tm, n // tn),
        in_specs=[pl.BlockSpec((tn, k), lambda i, j: (j, 0)),
                  pl.BlockSpec((tm, k), lambda i, j: (i, 0))],
        out_specs=pl.BlockSpec((tn, tm), lambda i, j: (j, i)),
        out_shape=jax.ShapeDtypeStruct((n, s), out_dtype),
        compiler_params=_cparams("parallel", "arbitrary"),
        name=name,
    )(wt, a)


def _log_sigmoid(x):
    return jnp.minimum(x, 0.0) - jnp.log1p(jnp.exp(-jnp.abs(x)))


def _gates_body(u_ref, wc_ref, bc_ref, gc_ref, gr_ref):
    gc = _dot(u_ref[...], wc_ref[...]) + bc_ref[...]
    lane = lax.broadcasted_iota(jnp.int32, gc.shape, 1)
    gc = jnp.where(lane >= A_HEADS, _log_sigmoid(gc), gc)
    gc_ref[...] = gc
    gr_ref[...] = gc.T[0:gr_ref.shape[0], :]


def _gates(u, wc, bc, *, tm=1024):
    s, d = u.shape
    return pl.pallas_call(
        _gates_body,
        grid=(s // tm,),
        in_specs=[pl.BlockSpec((tm, d), lambda i: (i, 0)),
                  pl.BlockSpec((d, 128), lambda i: (0, 0)),
                  pl.BlockSpec((1, 128), lambda i: (0, 0))],
        out_specs=[pl.BlockSpec((tm, 128), lambda i: (i, 0)),
                   pl.BlockSpec((16, tm), lambda i: (0, i))],
        out_shape=[jax.ShapeDtypeStruct((s, 128), F32),
                   jax.ShapeDtypeStruct((16, s), F32)],
        compiler_params=_cparams("parallel"),
        name="mlstm_gates",
    )(u, wc, bc)


A_EXT = 16


def _mlstm_body(vt_ref, qt_ref, k_ref, gc_ref, gr_ref, ot_ref, nw_ref, y_ref, c_ref, m_ref, *, chunk):
    L = chunk

    @pl.when(pl.program_id(0) == 0)
    def _():
        c_ref[...] = jnp.zeros_like(c_ref)
        m_ref[...] = jnp.full_like(m_ref, NEG)

    key = lax.broadcasted_iota(jnp.int32, (L, L), 0)
    qry = lax.broadcasted_iota(jnp.int32, (L, L), 1)
    causal = key <= qry
    tril = (qry <= key).astype(F32)
    triu = causal.astype(F32)
    gc = gc_ref[...]
    gr = gr_ref[...]
    cum_c = _dot(tril, gc, precision=lax.Precision.HIGHEST)
    cum_r = _dot(gr, triu, precision=lax.Precision.HIGHEST)
    ones_rows = (lax.broadcasted_iota(jnp.int32, (A_EXT, L), 0) == 0).astype(BF16)

    for h in range(A_HEADS):
        g_col = gc[:, h:h + 1] - cum_c[:, A_HEADS + h:A_HEADS + h + 1]
        brow = cum_r[A_HEADS + h:A_HEADS + h + 1, :]
        irow = gr[h:h + 1, :]
        btot = brow[:, L - 1:L]
        m_prev = m_ref[h][0:1, 0:1]

        e = jnp.where(causal, g_col, NEG)
        m_rel = jnp.maximum(jnp.max(e, axis=0, keepdims=True), m_prev)
        w_intra = jnp.exp(e - m_rel)
        w_inter = jnp.exp(m_prev - m_rel)

        qt = qt_ref[h * A_DQK:(h + 1) * A_DQK, :]
        kh = k_ref[:, h * A_DQK:(h + 1) * A_DQK]
        vx = jnp.concatenate([vt_ref[h * A_DV:(h + 1) * A_DV, :], ones_rows], axis=0)
        c_prev = c_ref[h]

        sc = _dot(kh, qt) * w_intra
        r = _dot(vx, sc.astype(BF16)) + w_inter * _dot(c_prev.astype(BF16), qt)
        den = r[A_DV:A_DV + 1, :]
        h_out = r[0:A_DV, :] / jnp.maximum(jnp.abs(den), jnp.exp(-(brow + m_rel)))

        d_state = btot - brow + irow
        m_new = jnp.maximum(btot + m_prev, jnp.max(d_state, axis=1, keepdims=True))
        w_state = jnp.exp(d_state - m_new)
        w_prev = jnp.exp(btot + m_prev - m_new)
        c_ref[h] = w_prev * c_prev + _dot((vx.astype(F32) * w_state).astype(BF16), kh)
        m_ref[h] = jnp.broadcast_to(m_new, m_ref.shape[1:])

        sl = slice(h * A_DV, (h + 1) * A_DV)
        hn = h_out * lax.rsqrt(jnp.mean(h_out * h_out, axis=0, keepdims=True) + NORM_EPS) * nw_ref[sl, :]
        y_ref[sl, :] = (_sigmoid(ot_ref[sl, :]) * hn).astype(BF16)


def _mlstm(vqt, kk, gc, gr, ot, nw_full, *, chunk=256):
    s = kk.shape[0]
    L = chunk
    return pl.pallas_call(
        functools.partial(_mlstm_body, chunk=L),
        grid=(s // L,),
        in_specs=[pl.BlockSpec((A_WIDTH, L), lambda c: (0, c)),
                  pl.BlockSpec((A_HEADS * A_DQK, L), lambda c: (2, c)),
                  pl.BlockSpec((L, A_HEADS * A_DQK), lambda c: (c, 0)),
                  pl.BlockSpec((L, 128), lambda c: (c, 0)),
                  pl.BlockSpec((16, L), lambda c: (0, c)),
                  pl.BlockSpec((A_WIDTH, L), lambda c: (0, c)),
                  pl.BlockSpec((A_WIDTH, L), lambda c: (0, 0))],
        out_specs=pl.BlockSpec((A_WIDTH, L), lambda c: (0, c)),
        out_shape=jax.ShapeDtypeStruct((A_WIDTH, s), BF16),
        scratch_shapes=[pltpu.VMEM((A_HEADS, A_DV + A_EXT, A_DQK), F32),
                        pltpu.VMEM((A_HEADS, 8, 128), F32)],
        compiler_params=_cparams("arbitrary"),
        name="mlstm",
    )(vqt, vqt, kk, gc, gr, ot, nw_full)


FLASH_BLOCK = 512
FLASH_QTILE = 1024


def _mla_prep_body(p3_ref, qn_ref, kvn_ref, wqt_ref, wk_ref, wvt_ref, cs_ref, cst_ref,
                   qt_out, k_out, vt_out):
    p3 = p3_ref[...]
    cq = _rms(p3[:, 0:B_Q_LORA], qn_ref[...]).astype(BF16)
    ckv = _rms(p3[:, 512:768], kvn_ref[...]).astype(BF16)
    cs = cs_ref[...]
    cst = cst_ref[...]
    low = lax.broadcasted_iota(jnp.int32, cs.shape, 1) < B_ROPE

    tt = p3[:, 384:512] * cs
    kr = jnp.where(low, tt + pltpu.roll(tt, B_ROPE, axis=1), 0.0).astype(BF16)
    kn = _dot(ckv, wk_ref[...])
    qt = _dot_nt(wqt_ref[...], cq)
    vt = _dot_nt(wvt_ref[...], ckv)
    zeros = jnp.zeros((B_ROPE, cst.shape[1]), BF16)
    for h in range(B_HEADS):
        k_out[h, :, 0:128] = kn[:, h * 128:(h + 1) * 128].astype(BF16)
        k_out[h, :, 128:256] = kr
        qt_out[h, 0:128, :] = qt[h * 128:(h + 1) * 128, :].astype(BF16)
        tq = qt[1024 + h * 128:1024 + (h + 1) * 128, :] * cst
        qt_out[h, 128:192, :] = (tq[0:B_ROPE, :] + tq[B_ROPE:2 * B_ROPE, :]).astype(BF16)
        qt_out[h, 192:256, :] = zeros
        vt_out[h, 0] = vt[h * 128:(h + 1) * 128, :].astype(BF16)


def _mla_prep(p3, qn, kvn, wqt, wk, wvt, cs, cst):
    s = p3.shape[0]
    tm = FLASH_BLOCK
    return pl.pallas_call(
        _mla_prep_body,
        grid=(s // tm,),
        in_specs=[pl.BlockSpec((tm, 768), lambda i: (i, 0)),
                  pl.BlockSpec((1, B_Q_LORA), lambda i: (0, 0)),
                  pl.BlockSpec((1, B_KV_LORA), lambda i: (0, 0)),
                  pl.BlockSpec((2048, B_Q_LORA), lambda i: (0, 0)),
                  pl.BlockSpec((B_KV_LORA, 1024), lambda i: (0, 0)),
                  pl.BlockSpec((1024, B_KV_LORA), lambda i: (0, 0)),
                  pl.BlockSpec((tm, 128), lambda i: (i, 0)),
                  pl.BlockSpec((128, tm), lambda i: (0, i))],
        out_specs=[pl.BlockSpec((B_HEADS, B_DK_PAD, tm), lambda i: (0, 0, i)),
                   pl.BlockSpec((B_HEADS, tm, B_DK_PAD), lambda i: (0, i, 0)),
                   pl.BlockSpec((B_HEADS, 1, B_DV, tm), lambda i: (0, i, 0, 0))],
        out_shape=[jax.ShapeDtypeStruct((B_HEADS, B_DK_PAD, s), BF16),
                   jax.ShapeDtypeStruct((B_HEADS, s, B_DK_PAD), BF16),
                   jax.ShapeDtypeStruct((B_HEADS, s // tm, B_DV, tm), BF16)],
        compiler_params=_cparams("parallel"),
        name="mla_prep",
    )(p3, qn, kvn, wqt, wk, wvt, cs, cst)


def _flash_body(qt_ref, k_ref, vt_ref, o_ref, sa_ref, sb_ref, mxa_ref, mxb_ref, acc_ref, m_ref, l_ref,
                *, blk, tq):
    qi = pl.program_id(1)
    n_sub = tq // blk
    qt = qt_ref[...]
    slot_a = (sa_ref, mxa_ref)
    slot_b = (sb_ref, mxb_ref)

    def scores(kb, slot, diag=None):
        s_ref, mx_ref = slot
        off = pl.multiple_of(kb * blk, blk)
        s = _dot(k_ref[pl.ds(off, blk), :], qt)
        if diag is not None:
            key = lax.broadcasted_iota(jnp.int32, (blk, tq), 0) + diag * blk
            qry = lax.broadcasted_iota(jnp.int32, (blk, tq), 1)
            s = jnp.where(key <= qry, s, NEG)
        s_ref[...] = s
        mx_ref[...] = jnp.broadcast_to(jnp.max(s, axis=0, keepdims=True), mx_ref.shape)

    def softmax_pv(kb, slot):
        s_ref, mx_ref = slot
        m_prev = m_ref[0:1, :]
        m_new = jnp.maximum(m_prev, mx_ref[0:1, :])
        alpha = jnp.exp2(m_prev - m_new)
        p = jnp.exp2(s_ref[...] - m_new)
        l_ref[...] = jnp.broadcast_to(alpha * l_ref[0:1, :] + jnp.sum(p, axis=0, keepdims=True), l_ref.shape)
        m_ref[...] = jnp.broadcast_to(m_new, m_ref.shape)
        acc_ref[...] = alpha * acc_ref[...] + _dot(vt_ref[kb], p.astype(BF16))

    m_ref[...] = jnp.full_like(m_ref, NEG)
    l_ref[...] = jnp.zeros_like(l_ref)
    acc_ref[...] = jnp.zeros_like(acc_ref)

    def pair(j, kb_cur):
        scores(j, slot_b)
        softmax_pv(kb_cur, slot_a)
        scores(j + 1, slot_a)
        softmax_pv(j, slot_b)
        return j + 1

    def quad(jj, kb_cur):
        return pair(4 * jj + 2, pair(4 * jj, kb_cur))

    d0 = n_sub * qi
    scores(d0, slot_b, diag=0)
    scores(d0 + 1, slot_a, diag=1)
    softmax_pv(d0, slot_b)

    n_full = n_sub * qi
    n_quad = n_full // 4
    kb_cur = lax.fori_loop(0, n_quad, quad, d0 + 1)

    @pl.when(n_full - 4 * n_quad == 0)
    def _():
        softmax_pv(kb_cur, slot_a)

    @pl.when(n_full - 4 * n_quad == 2)
    def _():
        softmax_pv(pair(n_full - 2, kb_cur), slot_a)

    o = acc_ref[...] / l_ref[0:1, :]
    o_ref[...] = o.T.astype(BF16)


def _flash(qt, k, vt):
    nh, dk, s = qt.shape
    blk, tq = FLASH_BLOCK, FLASH_QTILE
    assert tq == 2 * blk
    return pl.pallas_call(
        functools.partial(_flash_body, blk=blk, tq=tq),
        grid=(nh, s // tq),
        in_specs=[pl.BlockSpec((None, dk, tq), lambda h, i: (h, 0, i)),
                  pl.BlockSpec((None, s, dk), lambda h, i: (h, 0, 0)),
                  pl.BlockSpec((None, s // blk, B_DV, blk), lambda h, i: (h, 0, 0, 0))],
        out_specs=pl.BlockSpec((tq, B_DV), lambda h, i: (i, h)),
        out_shape=jax.ShapeDtypeStruct((s, nh * B_DV), BF16),
        scratch_shapes=[pltpu.VMEM((blk, tq), F32),
                        pltpu.VMEM((blk, tq), F32),
                        pltpu.VMEM((8, tq), F32),
                        pltpu.VMEM((8, tq), F32),
                        pltpu.VMEM((B_DV, tq), F32),
                        pltpu.VMEM((8, tq), F32),
                        pltpu.VMEM((8, tq), F32)],
        compiler_params=_cparams("parallel", "arbitrary"),
        name="mla_flash",
    )(qt, k, vt)


def _lru_body(cx_ref, cw_ref, cb_ref, wax_ref, ba_ref, bx_ref, lam_ref, y_ref,
              xs_ref, h_ref, *, tb):
    T = tb

    @pl.when(pl.program_id(0) == 0)
    def _():
        xs_ref[0:8, :] = jnp.zeros((8, C_WIDTH), F32)
        h_ref[...] = jnp.zeros_like(h_ref)

    x = cx_ref[...]
    xs_ref[8:8 + T, :] = x
    w = cw_ref[...]
    xc = (w[3:4, :] * x + w[2:3, :] * xs_ref[7:7 + T, :] + w[1:2, :] * xs_ref[6:6 + T, :]
          + w[0:1, :] * xs_ref[5:5 + T, :] + cb_ref[...])
    xs_ref[0:8, :] = x[T - 8:T, :]

    ra, rx = [], []
    for b in range(C_BLOCKS):
        xb = xc[:, b * C_BLOCK_DIM:(b + 1) * C_BLOCK_DIM].astype(BF16)
        g = _dot(xb, wax_ref[b])
        ra.append(g[:, :C_BLOCK_DIM])
        rx.append(g[:, C_BLOCK_DIM:])
    r = _sigmoid(jnp.concatenate(ra, axis=1) + ba_ref[...])
    gi = _sigmoid(jnp.concatenate(rx, axis=1) + bx_ref[...])
    log_a = -C_POW * r * _softplus(-lam_ref[...])
    a = jnp.exp(log_a)
    u = jnp.sqrt(-jnp.tanh(log_a) * (a * a + 1.0)) * (gi * xc)

    G = T // 8
    a = a.reshape(G, 8, C_WIDTH)
    u = u.reshape(G, 8, C_WIDTH)
    sub = lax.broadcasted_iota(jnp.int32, (G, 8, C_WIDTH), 1)
    for d in (1, 2, 4):
        keep = sub >= d
        a_s = jnp.where(keep, pltpu.roll(a, d, axis=1), 1.0)
        u_s = jnp.where(keep, pltpu.roll(u, d, axis=1), 0.0)
        u = a * u_s + u
        a = a * a_s
    h_in = h_ref[0:1, :]
    groups = []
    for g in range(G):
        hg = a[g] * h_in + u[g]
        groups.append(hg)
        h_in = hg[7:8, :]
    h_ref[...] = jnp.broadcast_to(h_in, h_ref.shape)
    y_ref[...] = jnp.concatenate(groups, axis=0).astype(BF16)


def _lru(p2, cw, cb, wax, ba, bx, lam, *, tb=256):
    s = p2.shape[0]
    vec = pl.BlockSpec((1, C_WIDTH), lambda c: (0, 0))
    return pl.pallas_call(
        functools.partial(_lru_body, tb=tb),
        grid=(s // tb,),
        in_specs=[pl.BlockSpec((tb, C_WIDTH), lambda c: (c, 0)),
                  pl.BlockSpec((C_CONV, C_WIDTH), lambda c: (0, 0)),
                  vec,
                  pl.BlockSpec((C_BLOCKS, C_BLOCK_DIM, 2 * C_BLOCK_DIM), lambda c: (0, 0, 0)),
                  vec, vec, vec],
        out_specs=pl.BlockSpec((tb, C_WIDTH), lambda c: (c, 0)),
        out_shape=jax.ShapeDtypeStruct((s, C_WIDTH), BF16),
        scratch_shapes=[pltpu.VMEM((tb + 8, C_WIDTH), F32),
                        pltpu.VMEM((8, C_WIDTH), F32)],
        compiler_params=_cparams("arbitrary"),
        name="rglru",
    )(p2, cw, cb, wax, ba, bx, lam)


def _merge_body(x_ref, u_ref, ya_ref, yb_ref, yc_ref, wga_ref, wgb_ref, wgc_ref, wb_ref, wo_ref, o_ref,
                *, n_tiles):
    n = pl.program_id(1)

    def out_tile():
        u = u_ref[...]
        z = None
        branch = (_dot_tn(ya_ref[...], wb_ref[0]), _dot(yb_ref[...], wb_ref[1]), _dot(yc_ref[...], wb_ref[2]))
        for p, wg_ref in zip(branch, (wga_ref, wgb_ref, wgc_ref)):
            t = _sigmoid(_dot(u, wg_ref[...])) * p
            z = t if z is None else z + t
        return _dot(z.astype(BF16), wo_ref[...])

    @pl.when(n == 0)
    def _():
        o_ref[...] = out_tile()

    @pl.when(jnp.logical_and(n > 0, n < n_tiles - 1))
    def _():
        o_ref[...] += out_tile()

    @pl.when(n == n_tiles - 1)
    def _():
        o_ref[...] = x_ref[...] + (o_ref[...] + out_tile())


def _merge(x, u, ya, yb, yc, wg, wb, wo, *, layer, tm=512):
    s, d = x.shape
    n_tiles, tn = wb.shape[2], wb.shape[4]
    br = pl.BlockSpec((tm, 1024), lambda i, n: (i, 0))
    gate_specs = [pl.BlockSpec((None, d, tn), functools.partial(lambda i, n, j: (j * n_tiles + n, 0, 0), j=j))
                  for j in range(N_BRANCH)]
    return pl.pallas_call(
        functools.partial(_merge_body, n_tiles=n_tiles),
        grid=(s // tm, n_tiles),
        in_specs=[pl.BlockSpec((tm, d), lambda i, n: (i, 0)),
                  pl.BlockSpec((tm, d), lambda i, n: (i, 0)),
                  pl.BlockSpec((1024, tm), lambda i, n: (0, i)),
                  br, br,
                  *gate_specs,
                  pl.BlockSpec((None, N_BRANCH, None, 1024, tn), lambda i, n: (layer, 0, n, 0, 0)),
                  pl.BlockSpec((None, tn, d), lambda i, n: (layer, n, 0))],
        out_specs=pl.BlockSpec((tm, d), lambda i, n: (i, 0)),
        out_shape=jax.ShapeDtypeStruct((s, d), F32),
        compiler_params=_cparams("parallel", "arbitrary"),
        name="merge",
    )(x, u, ya, yb, yc, wg, wg, wg, wb, wo)


def _rot_cols(w):
    half = B_ROPE // 2
    return jnp.concatenate([-w[..., half:], w[..., :half]], axis=-1)


def _prep_layer(w_in, gate_bias, w_uq, w_ukv, lru_w_a, lru_w_x):
    d = w_in.shape[0]
    w_vq_t = jnp.concatenate([w_in[:, _OFF_V:_OFF_I], w_in[:, _OFF_Q:_OFF_K] * (A_DQK ** -0.5)], axis=1).T.astype(BF16)
    w_k = w_in[:, _OFF_K:_OFF_V].astype(BF16)
    w_o_t = w_in[:, _OFF_O:_OFF_CQ].T.astype(BF16)
    w_cx = w_in[:, _OFF_CX:_OFF_G].astype(BF16)
    w_kr = w_in[:, _OFF_KR:_OFF_CX]
    w3 = jnp.concatenate([w_in[:, _OFF_CQ:_OFF_CKV], w_kr, _rot_cols(w_kr),
                          w_in[:, _OFF_CKV:_OFF_KR]], axis=1).astype(BF16)
    w_if = w_in[:, _OFF_I:_OFF_O]
    wc = jnp.concatenate([w_if, jnp.zeros((d, 120), F32)], axis=1).astype(BF16)
    bc = jnp.concatenate([gate_bias, jnp.zeros((120,), F32)])[None, :]

    scale = (B_NOPE + B_ROPE) ** -0.5 * LOG2_E
    wq3 = w_uq.reshape(B_Q_LORA, B_HEADS, B_NOPE + B_ROPE)
    q_nope = wq3[:, :, :B_NOPE].reshape(B_Q_LORA, B_HEADS * B_NOPE)
    q_rope = wq3[:, :, B_NOPE:]
    q_rr = jnp.concatenate([q_rope, _rot_cols(q_rope)], axis=-1).reshape(B_Q_LORA, B_HEADS * 128)
    wqt = (jnp.concatenate([q_nope, q_rr], axis=1) * scale).T.astype(BF16)
    wkv3 = w_ukv.reshape(B_KV_LORA, B_HEADS, B_NOPE + B_DV)
    wk = wkv3[:, :, :B_NOPE].reshape(B_KV_LORA, -1).astype(BF16)
    wvt = wkv3[:, :, B_NOPE:].reshape(B_KV_LORA, -1).T.astype(BF16)
    wax = jnp.concatenate([lru_w_a, lru_w_x], axis=-1).astype(BF16)
    wg = (w_in[:, _OFF_G:].astype(BF16).reshape(d, N_BRANCH * D_MODEL // MERGE_TILE, MERGE_TILE)
          .transpose(1, 0, 2))
    return dict(w_vq_t=w_vq_t, w_k=w_k, w_o_t=w_o_t, w_cx=w_cx, w3=w3, wc=wc, bc=bc, wqt=wqt, wk=wk, wvt=wvt, wax=wax, wg=wg)


def kernel(x, ffn1_norm, ffn1_w_gate, ffn1_w_up, ffn1_w_down, mix_norm, w_in, mlstm_gate_bias, mlstm_out_norm, mla_q_norm, mla_w_uq, mla_kv_norm, mla_w_ukv, lru_conv_w, lru_conv_b, lru_w_a, lru_b_a, lru_w_x, lru_b_x, lru_lambda, w_branch, w_out, ffn2_norm, ffn2_w_gate, ffn2_w_up, ffn2_w_down, final_norm):
    b, s, d = x.shape
    depth = w_in.shape[0]
    pos = jnp.arange(s, dtype=F32)
    inv_freq = jnp.power(ROPE_BASE, -jnp.arange(0, B_ROPE, 2, dtype=F32) / B_ROPE)
    ang = pos[:, None] * inv_freq[None, :]
    cos, sin = jnp.cos(ang), jnp.sin(ang)
    cs = jnp.concatenate([cos, cos, sin, sin], axis=1)
    cst = cs.T

    f1g, f1u = (_cast_bf16_col_tiles(w, tf=FFN_TILE, rows=256) for w in (ffn1_w_gate, ffn1_w_up))
    f2g, f2u = (_cast_bf16_col_tiles(w, tf=FFN_TILE, rows=256) for w in (ffn2_w_gate, ffn2_w_up))
    f1d, f2d = _cast_bf16(ffn1_w_down, rows=512), _cast_bf16(ffn2_w_down, rows=512)
    wb_all = _cast_bf16_col_tiles(w_branch.reshape(depth * N_BRANCH, 1024, d), tf=MERGE_TILE, rows=512)
    wb_all = wb_all.reshape(depth, N_BRANCH, d // MERGE_TILE, 1024, MERGE_TILE)
    wo_all = _cast_bf16(w_out, rows=512)
    layer_w = [_prep_layer(w_in[l], mlstm_gate_bias[l], mla_w_uq[l], mla_w_ukv[l], lru_w_a[l], lru_w_x[l])
               for l in range(depth)]

    outs = []
    for bi in range(b):
        xb = x[bi]
        for l in range(depth):
            w = layer_w[l]
            xb, u = _ffn(xb, ffn1_norm[l][None], f1g, f1u, f1d, mix_norm[l][None], layer=l, mode="mix")
            vqt = _matmul_t(w["w_vq_t"], u, BF16, name="proj_vq_t")
            kk = _matmul(u, w["w_k"], BF16, name="proj_k")
            ot = _matmul_t(w["w_o_t"], u, F32, name="proj_o_t")
            cx = _matmul(u, w["w_cx"], F32, tn=1024, name="proj_cx")
            p3 = _matmul(u, w["w3"], F32, tn=768, name="proj_mla")
            gc, gr = _gates(u, w["wc"], w["bc"])
            nw_full = jnp.broadcast_to(mlstm_out_norm[l][:, None], (A_WIDTH, 256))
            ya = _mlstm(vqt, kk, gc, gr, ot, nw_full)
            qt, kh, vt = _mla_prep(p3, mla_q_norm[l][None], mla_kv_norm[l][None], w["wqt"], w["wk"],
                                   w["wvt"], cs, cst)
            yb = _flash(qt, kh, vt)
            yc = _lru(cx, lru_conv_w[l], lru_conv_b[l][None], w["wax"], lru_b_a[l][None],
                      lru_b_x[l][None], lru_lambda[l][None])
            xb = _merge(xb, u, ya, yb, yc, w["wg"], wb_all, wo_all, layer=l)
            last = l == depth - 1
            xb = _ffn(xb, ffn2_norm[l][None], f2g, f2u, f2d, final_norm[None] if last else ffn2_norm[l][None],
                      layer=l, mode="final" if last else "plain")
        outs.append(xb[None])
    return outs[0] if b == 1 else jnp.concatenate(outs, axis=0)
```

```python
import functools

import jax
import jax.numpy as jnp
from jax import lax
from jax.experimental import pallas as pl
from jax.experimental.pallas import tpu as pltpu

F32 = jnp.float32
BF16 = jnp.bfloat16

D_MODEL = 2048
D_FF = 5632
NORM_EPS = 1e-6
NEG = -1e30
LOG2_E = 1.4426950408889634

A_HEADS = 4
A_DQK = 128
A_DV = 256
A_WIDTH = A_HEADS * A_DV

B_HEADS = 8
B_Q_LORA = 384
B_KV_LORA = 256
B_NOPE = 128
B_ROPE = 64
B_DV = 128
B_DK_PAD = 256
ROPE_BASE = 10000.0

C_WIDTH = 1024
C_BLOCKS = 8
C_BLOCK_DIM = C_WIDTH // C_BLOCKS
C_CONV = 4
C_POW = 8.0

N_BRANCH = 3

_OFF_Q = 0
_OFF_K = 512
_OFF_V = 1024
_OFF_I = 2048
_OFF_O = 2056
_OFF_CQ = 3080
_OFF_CKV = 3464
_OFF_KR = 3720
_OFF_CX = 3784
_OFF_G = 4808

VMEM_LIMIT_BYTES = 56 * 1024 * 1024


def _cparams(*sem):
    return pltpu.CompilerParams(dimension_semantics=sem, vmem_limit_bytes=VMEM_LIMIT_BYTES)


def _sigmoid(x):
    return 1.0 / (1.0 + jnp.exp(-x))


def _softplus(x):
    return jnp.maximum(x, 0.0) + jnp.log1p(jnp.exp(-jnp.abs(x)))


def _rms(x, g):
    return x * lax.rsqrt(jnp.mean(x * x, axis=-1, keepdims=True) + NORM_EPS) * g


def _dot(a, b, **kw):
    return jnp.dot(a, b, preferred_element_type=F32, **kw)


def _dot_nt(a, b):
    return lax.dot_general(a, b, (((1,), (1,)), ((), ())), preferred_element_type=F32)


def _dot_tn(a, b):
    return lax.dot_general(a, b, (((0,), (0,)), ((), ())), preferred_element_type=F32)


def _ffn_body(x_ref, g_ref, wg_ref, wu_ref, wd_ref, g2_ref, *rest, n_ff, mode):
    if mode == "mix":
        o_ref, u_ref, xn_ref = rest
    else:
        o_ref, xn_ref = rest
    j = pl.program_id(1)

    def down(xn):
        h1 = _dot(xn, wg_ref[...])
        h2 = _dot(xn, wu_ref[...])
        h = (h1 * _sigmoid(h1) * h2).astype(BF16)
        return _dot(h, wd_ref[...])

    @pl.when(j == 0)
    def _():
        xn = _rms(x_ref[...], g_ref[...]).astype(BF16)
        xn_ref[...] = xn
        o_ref[...] = down(xn)

    @pl.when(jnp.logical_and(j > 0, j < n_ff - 1))
    def _():
        o_ref[...] += down(xn_ref[...])

    @pl.when(j == n_ff - 1)
    def _():
        y = x_ref[...] + 0.5 * (o_ref[...] + down(xn_ref[...]))
        if mode == "final":
            o_ref[...] = _rms(y, g2_ref[...])
        else:
            o_ref[...] = y
        if mode == "mix":
            u_ref[...] = _rms(y, g2_ref[...]).astype(BF16)


def _cast_body(x_ref, o_ref):
    o_ref[...] = x_ref[...].astype(o_ref.dtype)


def _cast_bf16(w, *, rows):
    shape = w.shape
    w2 = w.reshape(-1, shape[-1])
    r, c = w2.shape
    out = pl.pallas_call(
        _cast_body,
        grid=(r // rows,),
        in_specs=[pl.BlockSpec((rows, c), lambda i: (i, 0))],
        out_specs=pl.BlockSpec((rows, c), lambda i: (i, 0)),
        out_shape=jax.ShapeDtypeStruct((r, c), BF16),
        compiler_params=_cparams("parallel"),
        name="cast_bf16",
    )(w2)
    return out.reshape(shape)


def _cast_tiles_body(x_ref, o_ref):
    tf = o_ref.shape[2]
    for t in range(o_ref.shape[0]):
        o_ref[t] = x_ref[:, t * tf:(t + 1) * tf].astype(o_ref.dtype)


def _cast_bf16_col_tiles(w, *, tf, rows):
    n, r, c = w.shape
    nt = c // tf
    return pl.pallas_call(
        _cast_tiles_body,
        grid=(n, r // rows),
        in_specs=[pl.BlockSpec((None, rows, c), lambda l, i: (l, i, 0))],
        out_specs=pl.BlockSpec((None, nt, rows, tf), lambda l, i: (l, 0, i, 0)),
        out_shape=jax.ShapeDtypeStruct((n, nt, r, tf), BF16),
        compiler_params=_cparams("parallel", "parallel"),
        name="cast_bf16_tiles",
    )(w)


W_IN_CAST_ROWS = 592
FFN_TILE = 512
MERGE_TILE = 512


def _ffn(x, g, wg, wu, wd, g2, *, layer, mode, tm=512):
    s, d = x.shape
    n_ff, tf = wg.shape[1], wg.shape[3]
    out_shape = [jax.ShapeDtypeStruct((s, d), F32)]
    out_specs = [pl.BlockSpec((tm, d), lambda i, j: (i, 0))]
    if mode == "mix":
        out_shape.append(jax.ShapeDtypeStruct((s, d), BF16))
        out_specs.append(pl.BlockSpec((tm, d), lambda i, j: (i, 0)))
    res = pl.pallas_call(
        functools.partial(_ffn_body, n_ff=n_ff, mode=mode),
        grid=(s // tm, n_ff),
        in_specs=[
            pl.BlockSpec((tm, d), lambda i, j: (i, 0)),
            pl.BlockSpec((1, d), lambda i, j: (0, 0)),
            pl.BlockSpec((None, None, d, tf), lambda i, j: (layer, j, 0, 0)),
            pl.BlockSpec((None, None, d, tf), lambda i, j: (layer, j, 0, 0)),
            pl.BlockSpec((None, tf, d), lambda i, j: (layer, j, 0)),
            pl.BlockSpec((1, d), lambda i, j: (0, 0)),
        ],
        out_specs=out_specs,
        out_shape=out_shape,
        scratch_shapes=[pltpu.VMEM((tm, d), BF16)],
        compiler_params=_cparams("parallel", "arbitrary"),
        name="ffn_" + mode,
    )(x, g, wg, wu, wd, g2)
    return res if mode == "mix" else res[0]


def _mm_body(a_ref, w_ref, o_ref):
    o_ref[...] = _dot(a_ref[...], w_ref[...]).astype(o_ref.dtype)


def _matmul(a, w, out_dtype, *, tm=1024, tn=512, name="proj"):
    s, k = a.shape
    n = w.shape[1]
    tn = min(tn, n)
    return pl.pallas_call(
        _mm_body,
        grid=(s // tm, n // tn),
        in_specs=[pl.BlockSpec((tm, k), lambda i, j: (i, 0)),
                  pl.BlockSpec((k, tn), lambda i, j: (0, j))],
        out_specs=pl.BlockSpec((tm, tn), lambda i, j: (i, j)),
        out_shape=jax.ShapeDtypeStruct((s, n), out_dtype),
        compiler_params=_cparams("parallel", "arbitrary"),
        name=name,
    )(a, w)


def _mm_t_body(w_ref, a_ref, o_ref):
    o_ref[...] = _dot_nt(w_ref[...], a_ref[...]).astype(o_ref.dtype)


def _matmul_t(wt, a, out_dtype, *, tm=1024, tn=512, name="proj_t"):
    n, k = wt.shape
    s = a.shape[0]
    return pl.pallas_call(
        _mm_t_body,
        grid=(s // tm, n // tn),
        in_specs=[pl.BlockSpec((tn, k), lambda i, j: (j, 0)),
                  pl.BlockSpec((tm, k), lambda i, j: (i, 0))],
        out_specs=pl.BlockSpec((tn, tm), lambda i, j: (j, i)),
        out_shape=jax.ShapeDtypeStruct((n, s), out_dtype),
        compiler_params=_cparams("parallel", "arbitrary"),
        name=name,
    )(wt, a)


def _log_sigmoid(x):
    return jnp.minimum(x, 0.0) - jnp.log1p(jnp.exp(-jnp.abs(x)))


def _gates_body(u_ref, wc_ref, bc_ref, gc_ref, gr_ref):
    gc = _dot(u_ref[...], wc_ref[...]) + bc_ref[...]
    lane = lax.broadcasted_iota(jnp.int32, gc.shape, 1)
    gc = jnp.where(lane >= A_HEADS, _log_sigmoid(gc), gc)
    gc_ref[...] = gc
    gr_ref[...] = gc.T[0:gr_ref.shape[0], :]


def _gates(u, wc, bc, *, tm=1024):
    s, d = u.shape
    return pl.pallas_call(
        _gates_body,
        grid=(s // tm,),
        in_specs=[pl.BlockSpec((tm, d), lambda i: (i, 0)),
                  pl.BlockSpec((d, 128), lambda i: (0, 0)),
                  pl.BlockSpec((1, 128), lambda i: (0, 0))],
        out_specs=[pl.BlockSpec((tm, 128), lambda i: (i, 0)),
                   pl.BlockSpec((16, tm), lambda i: (0, i))],
        out_shape=[jax.ShapeDtypeStruct((s, 128), F32),
                   jax.ShapeDtypeStruct((16, s), F32)],
        compiler_params=_cparams("parallel"),
        name="mlstm_gates",
    )(u, wc, bc)


A_EXT = 16


def _mlstm_body(vt_ref, qt_ref, k_ref, gc_ref, gr_ref, ot_ref, nw_ref, y_ref, c_ref, m_ref, *, chunk):
    L = chunk

    @pl.when(pl.program_id(0) == 0)
    def _():
        c_ref[...] = jnp.zeros_like(c_ref)
        m_ref[...] = jnp.full_like(m_ref, NEG)

    key = lax.broadcasted_iota(jnp.int32, (L, L), 0)
    qry = lax.broadcasted_iota(jnp.int32, (L, L), 1)
    causal = key <= qry
    tril = (qry <= key).astype(F32)
    triu = causal.astype(F32)
    gc = gc_ref[...]
    gr = gr_ref[...]
    cum_c = _dot(tril, gc, precision=lax.Precision.HIGHEST)
    cum_r = _dot(gr, triu, precision=lax.Precision.HIGHEST)
    ones_rows = (lax.broadcasted_iota(jnp.int32, (A_EXT, L), 0) == 0).astype(BF16)

    for h in range(A_HEADS):
        g_col = gc[:, h:h + 1] - cum_c[:, A_HEADS + h:A_HEADS + h + 1]
        brow = cum_r[A_HEADS + h:A_HEADS + h + 1, :]
        irow = gr[h:h + 1, :]
        btot = brow[:, L - 1:L]
        m_prev = m_ref[h][0:1, 0:1]

        e = jnp.where(causal, g_col, NEG)
        m_rel = jnp.maximum(jnp.max(e, axis=0, keepdims=True), m_prev)
        w_intra = jnp.exp(e - m_rel)
        w_inter = jnp.exp(m_prev - m_rel)

        qt = qt_ref[h * A_DQK:(h + 1) * A_DQK, :]
        kh = k_ref[:, h * A_DQK:(h + 1) * A_DQK]
        vx = jnp.concatenate([vt_ref[h * A_DV:(h + 1) * A_DV, :], ones_rows], axis=0)
        c_prev = c_ref[h]

        sc = _dot(kh, qt) * w_intra
        r = _dot(vx, sc.astype(BF16)) + w_inter * _dot(c_prev.astype(BF16), qt)
        den = r[A_DV:A_DV + 1, :]
        h_out = r[0:A_DV, :] / jnp.maximum(jnp.abs(den), jnp.exp(-(brow + m_rel)))

        d_state = btot - brow + irow
        m_new = jnp.maximum(btot + m_prev, jnp.max(d_state, axis=1, keepdims=True))
        w_state = jnp.exp(d_state - m_new)
        w_prev = jnp.exp(btot + m_prev - m_new)
        c_ref[h] = w_prev * c_prev + _dot((vx.astype(F32) * w_state).astype(BF16), kh)
        m_ref[h] = jnp.broadcast_to(m_new, m_ref.shape[1:])

        sl = slice(h * A_DV, (h + 1) * A_DV)
        hn = h_out * lax.rsqrt(jnp.mean(h_out * h_out, axis=0, keepdims=True) + NORM_EPS) * nw_ref[sl, :]
        y_ref[sl, :] = (_sigmoid(ot_ref[sl, :]) * hn).astype(BF16)


def _mlstm(vqt, kk, gc, gr, ot, nw_full, *, chunk=256):
    s = kk.shape[0]
    L = chunk
    return pl.pallas_call(
        functools.partial(_mlstm_body, chunk=L),
        grid=(s // L,),
        in_specs=[pl.BlockSpec((A_WIDTH, L), lambda c: (0, c)),
                  pl.BlockSpec((A_HEADS * A_DQK, L), lambda c: (2, c)),
                  pl.BlockSpec((L, A_HEADS * A_DQK), lambda c: (c, 0)),
                  pl.BlockSpec((L, 128), lambda c: (c, 0)),
                  pl.BlockSpec((16, L), lambda c: (0, c)),
                  pl.BlockSpec((A_WIDTH, L), lambda c: (0, c)),
                  pl.BlockSpec((A_WIDTH, L), lambda c: (0, 0))],
        out_specs=pl.BlockSpec((A_WIDTH, L), lambda c: (0, c)),
        out_shape=jax.ShapeDtypeStruct((A_WIDTH, s), BF16),
        scratch_shapes=[pltpu.VMEM((A_HEADS, A_DV + A_EXT, A_DQK), F32),
                        pltpu.VMEM((A_HEADS, 8, 128), F32)],
        compiler_params=_cparams("arbitrary"),
        name="mlstm",
    )(vqt, vqt, kk, gc, gr, ot, nw_full)


FLASH_BLOCK = 512
FLASH_QTILE = 1024


def _mla_prep_body(p3_ref, qn_ref, kvn_ref, wqt_ref, wk_ref, wvt_ref, cs_ref, cst_ref,
                   qt_out, k_out, vt_out):
    p3 = p3_ref[...]
    cq = _rms(p3[:, 0:B_Q_LORA], qn_ref[...]).astype(BF16)
    ckv = _rms(p3[:, 512:768], kvn_ref[...]).astype(BF16)
    cs = cs_ref[...]
    cst = cst_ref[...]
    low = lax.broadcasted_iota(jnp.int32, cs.shape, 1) < B_ROPE

    tt = p3[:, 384:512] * cs
    kr = jnp.where(low, tt + pltpu.roll(tt, B_ROPE, axis=1), 0.0).astype(BF16)
    kn = _dot(ckv, wk_ref[...])
    qt = _dot_nt(wqt_ref[...], cq)
    vt = _dot_nt(wvt_ref[...], ckv)
    zeros = jnp.zeros((B_ROPE, cst.shape[1]), BF16)
    for h in range(B_HEADS):
        k_out[h, :, 0:128] = kn[:, h * 128:(h + 1) * 128].astype(BF16)
        k_out[h, :, 128:256] = kr
        qt_out[h, 0:128, :] = qt[h * 128:(h + 1) * 128, :].astype(BF16)
        tq = qt[1024 + h * 128:1024 + (h + 1) * 128, :] * cst
        qt_out[h, 128:192, :] = (tq[0:B_ROPE, :] + tq[B_ROPE:2 * B_ROPE, :]).astype(BF16)
        qt_out[h, 192:256, :] = zeros
        vt_out[h, 0] = vt[h * 128:(h + 1) * 128, :].astype(BF16)


def _mla_prep(p3, qn, kvn, wqt, wk, wvt, cs, cst):
    s = p3.shape[0]
    tm = FLASH_BLOCK
    return pl.pallas_call(
        _mla_prep_body,
        grid=(s // tm,),
        in_specs=[pl.BlockSpec((tm, 768), lambda i: (i, 0)),
                  pl.BlockSpec((1, B_Q_LORA), lambda i: (0, 0)),
                  pl.BlockSpec((1, B_KV_LORA), lambda i: (0, 0)),
                  pl.BlockSpec((2048, B_Q_LORA), lambda i: (0, 0)),
                  pl.BlockSpec((B_KV_LORA, 1024), lambda i: (0, 0)),
                  pl.BlockSpec((1024, B_KV_LORA), lambda i: (0, 0)),
                  pl.BlockSpec((tm, 128), lambda i: (i, 0)),
                  pl.BlockSpec((128, tm), lambda i: (0, i))],
        out_specs=[pl.BlockSpec((B_HEADS, B_DK_PAD, tm), lambda i: (0, 0, i)),
                   pl.BlockSpec((B_HEADS, tm, B_DK_PAD), lambda i: (0, i, 0)),
                   pl.BlockSpec((B_HEADS, 1, B_DV, tm), lambda i: (0, i, 0, 0))],
        out_shape=[jax.ShapeDtypeStruct((B_HEADS, B_DK_PAD, s), BF16),
                   jax.ShapeDtypeStruct((B_HEADS, s, B_DK_PAD), BF16),
                   jax.ShapeDtypeStruct((B_HEADS, s // tm, B_DV, tm), BF16)],
        compiler_params=_cparams("parallel"),
        name="mla_prep",
    )(p3, qn, kvn, wqt, wk, wvt, cs, cst)


def _flash_body(qt_ref, k_ref, vt_ref, o_ref, sa_ref, sb_ref, mxa_ref, mxb_ref, acc_ref, m_ref, l_ref,
                *, blk, tq):
    qi = pl.program_id(1)
    n_sub = tq // blk
    qt = qt_ref[...]
    slot_a = (sa_ref, mxa_ref)
    slot_b = (sb_ref, mxb_ref)

    def scores(kb, slot, diag=None):
        s_ref, mx_ref = slot
        off = pl.multiple_of(kb * blk, blk)
        s = _dot(k_ref[pl.ds(off, blk), :], qt)
        if diag is not None:
            key = lax.broadcasted_iota(jnp.int32, (blk, tq), 0) + diag * blk
            qry = lax.broadcasted_iota(jnp.int32, (blk, tq), 1)
            s = jnp.where(key <= qry, s, NEG)
        s_ref[...] = s
        mx_ref[...] = jnp.broadcast_to(jnp.max(s, axis=0, keepdims=True), mx_ref.shape)

    def softmax_pv(kb, slot):
        s_ref, mx_ref = slot
        m_prev = m_ref[0:1, :]
        m_new = jnp.maximum(m_prev, mx_ref[0:1, :])
        alpha = jnp.exp2(m_prev - m_new)
        p = jnp.exp2(s_ref[...] - m_new)
        l_ref[...] = jnp.broadcast_to(alpha * l_ref[0:1, :] + jnp.sum(p, axis=0, keepdims=True), l_ref.shape)
        m_ref[...] = jnp.broadcast_to(m_new, m_ref.shape)
        acc_ref[...] = alpha * acc_ref[...] + _dot(vt_ref[kb], p.astype(BF16))

    m_ref[...] = jnp.full_like(m_ref, NEG)
    l_ref[...] = jnp.zeros_like(l_ref)
    acc_ref[...] = jnp.zeros_like(acc_ref)

    def pair(j, kb_cur):
        scores(j, slot_b)
        softmax_pv(kb_cur, slot_a)
        scores(j + 1, slot_a)
        softmax_pv(j, slot_b)
        return j + 1

    def quad(jj, kb_cur):
        return pair(4 * jj + 2, pair(4 * jj, kb_cur))

    d0 = n_sub * qi
    scores(d0, slot_b, diag=0)
    scores(d0 + 1, slot_a, diag=1)
    softmax_pv(d0, slot_b)

    n_full = n_sub * qi
    n_quad = n_full // 4
    kb_cur = lax.fori_loop(0, n_quad, quad, d0 + 1)

    @pl.when(n_full - 4 * n_quad == 0)
    def _():
        softmax_pv(kb_cur, slot_a)

    @pl.when(n_full - 4 * n_quad == 2)
    def _():
        softmax_pv(pair(n_full - 2, kb_cur), slot_a)

    o = acc_ref[...] / l_ref[0:1, :]
    o_ref[...] = o.T.astype(BF16)


def _flash(qt, k, vt):
    nh, dk, s = qt.shape
    blk, tq = FLASH_BLOCK, FLASH_QTILE
    assert tq == 2 * blk
    return pl.pallas_call(
        functools.partial(_flash_body, blk=blk, tq=tq),
        grid=(nh, s // tq),
        in_specs=[pl.BlockSpec((None, dk, tq), lambda h, i: (h, 0, i)),
                  pl.BlockSpec((None, s, dk), lambda h, i: (h, 0, 0)),
                  pl.BlockSpec((None, s // blk, B_DV, blk), lambda h, i: (h, 0, 0, 0))],
        out_specs=pl.BlockSpec((tq, B_DV), lambda h, i: (i, h)),
        out_shape=jax.ShapeDtypeStruct((s, nh * B_DV), BF16),
        scratch_shapes=[pltpu.VMEM((blk, tq), F32),
                        pltpu.VMEM((blk, tq), F32),
                        pltpu.VMEM((8, tq), F32),
                        pltpu.VMEM((8, tq), F32),
                        pltpu.VMEM((B_DV, tq), F32),
                        pltpu.VMEM((8, tq), F32),
                        pltpu.VMEM((8, tq), F32)],
        compiler_params=_cparams("parallel", "arbitrary"),
        name="mla_flash",
    )(qt, k, vt)


def _lru_body(cx_ref, cw_ref, cb_ref, wax_ref, ba_ref, bx_ref, lam_ref, y_ref,
              xs_ref, h_ref, *, tb):
    T = tb

    @pl.when(pl.program_id(0) == 0)
    def _():
        xs_ref[0:8, :] = jnp.zeros((8, C_WIDTH), F32)
        h_ref[...] = jnp.zeros_like(h_ref)

    x = cx_ref[...]
    xs_ref[8:8 + T, :] = x
    w = cw_ref[...]
    xc = (w[3:4, :] * x + w[2:3, :] * xs_ref[7:7 + T, :] + w[1:2, :] * xs_ref[6:6 + T, :]
          + w[0:1, :] * xs_ref[5:5 + T, :] + cb_ref[...])
    xs_ref[0:8, :] = x[T - 8:T, :]

    ra, rx = [], []
    for b in range(C_BLOCKS):
        xb = xc[:, b * C_BLOCK_DIM:(b + 1) * C_BLOCK_DIM].astype(BF16)
        g = _dot(xb, wax_ref[b])
        ra.append(g[:, :C_BLOCK_DIM])
        rx.append(g[:, C_BLOCK_DIM:])
    r = _sigmoid(jnp.concatenate(ra, axis=1) + ba_ref[...])
    gi = _sigmoid(jnp.concatenate(rx, axis=1) + bx_ref[...])
    log_a = -C_POW * r * _softplus(-lam_ref[...])
    a = jnp.exp(log_a)
    u = jnp.sqrt(-jnp.tanh(log_a) * (a * a + 1.0)) * (gi * xc)

    G = T // 8
    a = a.reshape(G, 8, C_WIDTH)
    u = u.reshape(G, 8, C_WIDTH)
    sub = lax.broadcasted_iota(jnp.int32, (G, 8, C_WIDTH), 1)
    for d in (1, 2, 4):
        keep = sub >= d
        a_s = jnp.where(keep, pltpu.roll(a, d, axis=1), 1.0)
        u_s = jnp.where(keep, pltpu.roll(u, d, axis=1), 0.0)
        u = a * u_s + u
        a = a * a_s
    h_in = h_ref[0:1, :]
    groups = []
    for g in range(G):
        hg = a[g] * h_in + u[g]
        groups.append(hg)
        h_in = hg[7:8, :]
    h_ref[...] = jnp.broadcast_to(h_in, h_ref.shape)
    y_ref[...] = jnp.concatenate(groups, axis=0).astype(BF16)


def _lru(p2, cw, cb, wax, ba, bx, lam, *, tb=256):
    s = p2.shape[0]
    vec = pl.BlockSpec((1, C_WIDTH), lambda c: (0, 0))
    return pl.pallas_call(
        functools.partial(_lru_body, tb=tb),
        grid=(s // tb,),
        in_specs=[pl.BlockSpec((tb, C_WIDTH), lambda c: (c, 0)),
                  pl.BlockSpec((C_CONV, C_WIDTH), lambda c: (0, 0)),
                  vec,
                  pl.BlockSpec((C_BLOCKS, C_BLOCK_DIM, 2 * C_BLOCK_DIM), lambda c: (0, 0, 0)),
                  vec, vec, vec],
        out_specs=pl.BlockSpec((tb, C_WIDTH), lambda c: (c, 0)),
        out_shape=jax.ShapeDtypeStruct((s, C_WIDTH), BF16),
        scratch_shapes=[pltpu.VMEM((tb + 8, C_WIDTH), F32),
                        pltpu.VMEM((8, C_WIDTH), F32)],
        compiler_params=_cparams("arbitrary"),
        name="rglru",
    )(p2, cw, cb, wax, ba, bx, lam)


def _merge_body(x_ref, u_ref, ya_ref, yb_ref, yc_ref, wga_ref, wgb_ref, wgc_ref, wb_ref, wo_ref, o_ref,
                *, n_tiles):
    n = pl.program_id(1)

    def out_tile():
        u = u_ref[...]
        z = None
        branch = (_dot_tn(ya_ref[...], wb_ref[0]), _dot(yb_ref[...], wb_ref[1]), _dot(yc_ref[...], wb_ref[2]))
        for p, wg_ref in zip(branch, (wga_ref, wgb_ref, wgc_ref)):
            t = _sigmoid(_dot(u, wg_ref[...])) * p
            z = t if z is None else z + t
        return _dot(z.astype(BF16), wo_ref[...])

    @pl.when(n == 0)
    def _():
        o_ref[...] = out_tile()

    @pl.when(jnp.logical_and(n > 0, n < n_tiles - 1))
    def _():
        o_ref[...] += out_tile()

    @pl.when(n == n_tiles - 1)
    def _():
        o_ref[...] = x_ref[...] + (o_ref[...] + out_tile())


def _merge(x, u, ya, yb, yc, wg, wb, wo, *, layer, tm=512):
    s, d = x.shape
    n_tiles, tn = wb.shape[2], wb.shape[4]
    br = pl.BlockSpec((tm, 1024), lambda i, n: (i, 0))
    gate_specs = [pl.BlockSpec((None, d, tn), functools.partial(lambda i, n, j: (j * n_tiles + n, 0, 0), j=j))
                  for j in range(N_BRANCH)]
    return pl.pallas_call(
        functools.partial(_merge_body, n_tiles=n_tiles),
        grid=(s // tm, n_tiles),
        in_specs=[pl.BlockSpec((tm, d), lambda i, n: (i, 0)),
                  pl.BlockSpec((tm, d), lambda i, n: (i, 0)),
                  pl.BlockSpec((1024, tm), lambda i, n: (0, i)),
                  br, br,
                  *gate_specs,
                  pl.BlockSpec((None, N_BRANCH, None, 1024, tn), lambda i, n: (layer, 0, n, 0, 0)),
                  pl.BlockSpec((None, tn, d), lambda i, n: (layer, n, 0))],
        out_specs=pl.BlockSpec((tm, d), lambda i, n: (i, 0)),
        out_shape=jax.ShapeDtypeStruct((s, d), F32),
        compiler_params=_cparams("parallel", "arbitrary"),
        name="merge",
    )(x, u, ya, yb, yc, wg, wg, wg, wb, wo)


def _rot_cols(w):
    half = B_ROPE // 2
    return jnp.concatenate([-w[..., half:], w[..., :half]], axis=-1)


def _prep_layer(wq_rows, wt, gate_bias, w_uq, w_ukv, lru_w_a, lru_w_x):
    d = wt.shape[1]
    half = B_ROPE // 2
    w_vq_t = jnp.concatenate([wt[_OFF_V:_OFF_I], (wq_rows * (A_DQK ** -0.5)).astype(BF16)], axis=0)
    w_k = wt[_OFF_K:_OFF_V].T
    w_o_t = wt[_OFF_O:_OFF_CQ]
    w_cx = wt[_OFF_CX:_OFF_G].T
    w_kr = wt[_OFF_KR:_OFF_CX]
    w_kr_rot = jnp.concatenate([-w_kr[half:], w_kr[:half]], axis=0)
    w3 = jnp.concatenate([wt[_OFF_CQ:_OFF_CKV], w_kr, w_kr_rot, wt[_OFF_CKV:_OFF_KR]], axis=0).T
    wc = jnp.concatenate([wt[_OFF_I:_OFF_O], jnp.zeros((120, d), BF16)], axis=0).T
    bc = jnp.concatenate([gate_bias, jnp.zeros((120,), F32)])[None, :]

    scale = (B_NOPE + B_ROPE) ** -0.5 * LOG2_E
    wq3 = w_uq.reshape(B_Q_LORA, B_HEADS, B_NOPE + B_ROPE)
    q_nope = wq3[:, :, :B_NOPE].reshape(B_Q_LORA, B_HEADS * B_NOPE)
    q_rope = wq3[:, :, B_NOPE:]
    q_rr = jnp.concatenate([q_rope, _rot_cols(q_rope)], axis=-1).reshape(B_Q_LORA, B_HEADS * 128)
    wqt = (jnp.concatenate([q_nope, q_rr], axis=1) * scale).T.astype(BF16)
    wkv3 = w_ukv.reshape(B_KV_LORA, B_HEADS, B_NOPE + B_DV)
    wk = wkv3[:, :, :B_NOPE].reshape(B_KV_LORA, -1).astype(BF16)
    wvt = wkv3[:, :, B_NOPE:].reshape(B_KV_LORA, -1).T.astype(BF16)
    wax = jnp.concatenate([lru_w_a, lru_w_x], axis=-1).astype(BF16)
    wg = (wt[_OFF_G:].reshape(N_BRANCH * D_MODEL // MERGE_TILE, MERGE_TILE, d)
          .transpose(0, 2, 1))
    return dict(w_vq_t=w_vq_t, w_k=w_k, w_o_t=w_o_t, w_cx=w_cx, w3=w3, wc=wc, bc=bc, wqt=wqt, wk=wk, wvt=wvt, wax=wax, wg=wg)


def kernel(x, ffn1_norm, ffn1_w_gate, ffn1_w_up, ffn1_w_down, mix_norm, w_in, mlstm_gate_bias, mlstm_out_norm, mla_q_norm, mla_w_uq, mla_kv_norm, mla_w_ukv, lru_conv_w, lru_conv_b, lru_w_a, lru_b_a, lru_w_x, lru_b_x, lru_lambda, w_branch, w_out, ffn2_norm, ffn2_w_gate, ffn2_w_up, ffn2_w_down, final_norm):
    b, s, d = x.shape
    depth = w_in.shape[0]
    pos = jnp.arange(s, dtype=F32)
    inv_freq = jnp.power(ROPE_BASE, -jnp.arange(0, B_ROPE, 2, dtype=F32) / B_ROPE)
    ang = pos[:, None] * inv_freq[None, :]
    cos, sin = jnp.cos(ang), jnp.sin(ang)
    cs = jnp.concatenate([cos, cos, sin, sin], axis=1)
    cst = cs.T

    f1g, f1u = (_cast_bf16_col_tiles(w, tf=FFN_TILE, rows=256) for w in (ffn1_w_gate, ffn1_w_up))
    f2g, f2u = (_cast_bf16_col_tiles(w, tf=FFN_TILE, rows=256) for w in (ffn2_w_gate, ffn2_w_up))
    f1d, f2d = _cast_bf16(ffn1_w_down, rows=512), _cast_bf16(ffn2_w_down, rows=512)
    wb_all = _cast_bf16_col_tiles(w_branch.reshape(depth * N_BRANCH, 1024, d), tf=MERGE_TILE, rows=512)
    wb_all = wb_all.reshape(depth, N_BRANCH, d // MERGE_TILE, 1024, MERGE_TILE)
    wo_all = _cast_bf16(w_out, rows=512)
    w_in_t = jnp.swapaxes(w_in, 1, 2)
    w_in_t_bf = _cast_bf16(w_in_t, rows=W_IN_CAST_ROWS)
    layer_w = [_prep_layer(w_in_t[l, _OFF_Q:_OFF_K], w_in_t_bf[l], mlstm_gate_bias[l], mla_w_uq[l],
                           mla_w_ukv[l], lru_w_a[l], lru_w_x[l])
               for l in range(depth)]

    outs = []
    for bi in range(b):
        xb = x[bi]
        for l in range(depth):
            w = layer_w[l]
            xb, u = _ffn(xb, ffn1_norm[l][None], f1g, f1u, f1d, mix_norm[l][None], layer=l, mode="mix")
            vqt = _matmul_t(w["w_vq_t"], u, BF16, name="proj_vq_t")
            kk = _matmul(u, w["w_k"], BF16, name="proj_k")
            ot = _matmul_t(w["w_o_t"], u, F32, name="proj_o_t")
            cx = _matmul(u, w["w_cx"], F32, tn=1024, name="proj_cx")
            p3 = _matmul(u, w["w3"], F32, tn=768, name="proj_mla")
            gc, gr = _gates(u, w["wc"], w["bc"])
            nw_full = jnp.broadcast_to(mlstm_out_norm[l][:, None], (A_WIDTH, 256))
            ya = _mlstm(vqt, kk, gc, gr, ot, nw_full)
            qt, kh, vt = _mla_prep(p3, mla_q_norm[l][None], mla_kv_norm[l][None], w["wqt"], w["wk"],
                                   w["wvt"], cs, cst)
            yb = _flash(qt, kh, vt)
            yc = _lru(cx, lru_conv_w[l], lru_conv_b[l][None], w["wax"], lru_b_a[l][None],
                      lru_b_x[l][None], lru_lambda[l][None])
            xb = _merge(xb, u, ya, yb, yc, w["wg"], wb_all, wo_all, layer=l)
            last = l == depth - 1
            xb = _ffn(xb, ffn2_norm[l][None], f2g, f2u, f2d, final_norm[None] if last else ffn2_norm[l][None],
                      layer=l, mode="final" if last else "plain")
        outs.append(xb[None])
    return outs[0] if b == 1 else jnp.concatenate(outs, axis=0)
```
